```python
import math
import jax, jax.numpy as jnp
from jax import lax
import numpy as np

D_MODEL = 2048
BATCH = 8
SEQ = 2048
DEPTH = 4

N_HEADS = 16
HEAD_DIM = D_MODEL // N_HEADS
D_FF = 256 * ((8 * D_MODEL // 3 + 255) // 256)
N_MIXERS = 3
Q_BLOCK = 128
REL_BUCKETS = 32
REL_MAX_DIST = 2048
DILATED_BRANCHES = ((128, 1), (512, 4), (2048, 16))
RMS_EPS = 1e-6
FFN_HALF = 0.5

kernel_name = "hybrid_fox_dilated_stickbreaking_macaron"


def rms_norm(x, g):
    xf = x.astype(jnp.float32)
    y = xf * lax.rsqrt(jnp.mean(xf * xf, axis=-1, keepdims=True) + RMS_EPS)
    return (y * g.astype(jnp.float32)).astype(x.dtype)


def swiglu(x, w_gate, w_up, w_down):
    return (jax.nn.silu(x @ w_gate) * (x @ w_up)) @ w_down


def split_heads(t):
    b, s, _ = t.shape
    return t.reshape(b, s, N_HEADS, HEAD_DIM).transpose(0, 2, 1, 3)


def merge_heads(o):
    b, h, s, dh = o.shape
    return o.transpose(0, 2, 1, 3).reshape(b, s, h * dh)


def to_query_blocks(t):
    b, h, s = t.shape[:3]
    t = t.reshape((b, h, s // Q_BLOCK, Q_BLOCK) + t.shape[3:])
    return jnp.moveaxis(t, 2, 0)


def from_query_blocks(o):
    nb, b, h, q, dh = o.shape
    return jnp.moveaxis(o, 0, 2).reshape(b, h, nb * q, dh)


def t5_bucket(dist):
    max_exact = REL_BUCKETS // 2
    d = jnp.maximum(dist, 1).astype(jnp.float32)
    large = max_exact + (jnp.log(d / max_exact) / math.log(REL_MAX_DIST / max_exact)
                         * (REL_BUCKETS - max_exact)).astype(jnp.int32)
    large = jnp.minimum(large, REL_BUCKETS - 1)
    return jnp.where(dist < max_exact, dist, large)


def forgetting_attention(h, w_in, b_f):
    proj = h @ w_in
    q, k, v = [split_heads(t) for t in jnp.split(proj[..., :3 * D_MODEL], 3, axis=-1)]
    log_f = jax.nn.log_sigmoid((proj[..., 3 * D_MODEL:] + b_f).astype(jnp.float32))
    c = jnp.cumsum(log_f, axis=1).transpose(0, 2, 1)
    seq = h.shape[1]
    nb = seq // Q_BLOCK
    scale = HEAD_DIM ** -0.5
    kpos = jnp.arange(seq)

    def block(args):
        qb, cb, n = args
        qpos = n * Q_BLOCK + jnp.arange(Q_BLOCK)
        s = jnp.einsum('bhqd,bhkd->bhqk', qb, k).astype(jnp.float32) * scale
        s = s + cb[..., :, None] - c[..., None, :]
        s = jnp.where(kpos[None, :] <= qpos[:, None], s, -jnp.inf)
        p = jax.nn.softmax(s, axis=-1).astype(v.dtype)
        return jnp.einsum('bhqk,bhkd->bhqd', p, v)

    o = lax.map(block, (to_query_blocks(q), to_query_blocks(c), jnp.arange(nb)))
    return merge_heads(from_query_blocks(o))


def dilated_branch(q, k, v, rel_bias, window, dilation):
    b, h, seq, dh = q.shape
    span = window // dilation
    length = seq // dilation
    padded = -(-length // Q_BLOCK) * Q_BLOCK
    nb = padded // Q_BLOCK

    def strided(t):
        t = t.reshape(b, h, length, dilation, dh).transpose(0, 1, 3, 2, 4)
        t = jnp.pad(t, ((0, 0), (0, 0), (0, 0), (0, padded - length), (0, 0)))
        return t.reshape(b, h, dilation, nb, Q_BLOCK, dh)

    def with_prev(t):
        prev = jnp.pad(t, ((0, 0), (0, 0), (0, 0), (1, 0), (0, 0), (0, 0)))[:, :, :, :-1]
        return jnp.concatenate([prev, t], axis=-2)

    qs = strided(q)
    kb = with_prev(strided(k))
    vb = with_prev(strided(v))
    qi = jnp.arange(Q_BLOCK)[:, None]
    kj = jnp.arange(2 * Q_BLOCK)[None, :]
    off = Q_BLOCK + qi - kj
    band = (off >= 0) & (off <= span)
    key_idx = jnp.arange(nb)[:, None, None] * Q_BLOCK - Q_BLOCK + kj[None]
    valid = band[None] & (key_idx >= 0)
    bias = rel_bias[:, t5_bucket(jnp.clip(off, 0) * dilation)].astype(jnp.float32)
    s = jnp.einsum('bhrnqd,bhrnkd->bhrnqk', qs, kb).astype(jnp.float32) * (HEAD_DIM ** -0.5)
    s = jnp.where(valid, s + bias[:, None, None], -jnp.inf)
    m = jnp.max(s, axis=-1, keepdims=True)
    e = jnp.exp(s - m)
    den = jnp.sum(e, axis=-1, keepdims=True)
    lse = (m + jnp.log(den))[..., 0]
    o = jnp.einsum('bhrnqk,bhrnkd->bhrnqd', (e / den).astype(v.dtype), vb)

    def unstride(t):
        t = t.reshape((b, h, dilation, padded) + t.shape[5:])[:, :, :, :length]
        t = jnp.swapaxes(t, 2, 3)
        return t.reshape((b, h, seq) + t.shape[4:])

    return unstride(o), unstride(lse)


def dilated_attention(h, w_in, rel_bias):
    proj = h @ w_in
    q, k, v = [split_heads(t) for t in jnp.split(proj, 3, axis=-1)]
    outs, lses = [], []
    for window, dilation in DILATED_BRANCHES:
        o, lse = dilated_branch(q, k, v, rel_bias, window, dilation)
        outs.append(o)
        lses.append(lse)
    weights = jax.nn.softmax(jnp.stack(lses, axis=0), axis=0)
    o = sum(weights[g][..., None].astype(v.dtype) * outs[g] for g in range(len(outs)))
    return merge_heads(o)


def stick_breaking_attention(h, w_in):
    proj = h @ w_in
    q, k, v = [split_heads(t) for t in jnp.split(proj, 3, axis=-1)]
    seq = h.shape[1]
    nb = seq // Q_BLOCK
    scale = HEAD_DIM ** -0.5
    kpos = jnp.arange(seq)

    def block(args):
        qb, n = args
        qpos = n * Q_BLOCK + jnp.arange(Q_BLOCK)
        z = jnp.einsum('bhqd,bhkd->bhqk', qb, k).astype(jnp.float32) * scale
        causal = kpos[None, :] < qpos[:, None]
        log_beta = jax.nn.log_sigmoid(z)
        log_one_minus = jnp.where(causal, jax.nn.log_sigmoid(-z), 0.0)
        after = lax.cumsum(log_one_minus, axis=3, reverse=True) - log_one_minus
        a = jnp.where(causal, jnp.exp(log_beta + after), 0.0)
        return jnp.einsum('bhqk,bhkd->bhqd', a.astype(v.dtype), v)

    o = lax.map(block, (to_query_blocks(q), jnp.arange(nb)))
    return merge_heads(from_query_blocks(o))


def setup_inputs(seed: int = 0) -> dict:
    key = jax.random.key(seed)
    ks = jax.random.split(key, 14)
    f32 = jnp.float32
    D, F, H = D_MODEL, D_FF, N_HEADS

    def dense(k, shape, fan_in):
        return jax.random.normal(k, shape, f32) * fan_in ** -0.5

    return {
        "x": jax.random.normal(ks[0], (BATCH, SEQ, D), f32),
        "norm_g": 1.0 + 0.05 * jax.random.normal(ks[1], (DEPTH, 3, D), f32),
        "ffn_w_gate": dense(ks[2], (DEPTH, 2, D, F), D),
        "ffn_w_up": dense(ks[3], (DEPTH, 2, D, F), D),
        "ffn_w_down": dense(ks[4], (DEPTH, 2, F, D), F),
        "w_in_0": dense(ks[5], (D, 3 * D + H), D),
        "b_f_0": 2.0 + 0.1 * jax.random.normal(ks[6], (H,), f32),
        "w_in_1": dense(ks[7], (D, 3 * D), D),
        "w_in_2": dense(ks[8], (D, 3 * D), D),
        "w_in_3": dense(ks[9], (D, 3 * D + H), D),
        "b_f_3": 2.0 + 0.1 * jax.random.normal(ks[10], (H,), f32),
        "w_out": dense(ks[11], (DEPTH, D, D), D),
        "rel_bias": 0.5 * jax.random.normal(ks[12], (H, REL_BUCKETS), f32),
        "final_g": 1.0 + 0.05 * jax.random.normal(ks[13], (D,), f32),
    }


def reference(x, norm_g, ffn_w_gate, ffn_w_up, ffn_w_down, w_in_0, b_f_0, w_in_1, w_in_2,
              w_in_3, b_f_3, w_out, rel_bias, final_g):
    w_ins = (w_in_0, w_in_1, w_in_2, w_in_3)
    forget_biases = {0: b_f_0, 3: b_f_3}
    for i in range(DEPTH):
        x = x + FFN_HALF * swiglu(rms_norm(x, norm_g[i, 0]),
                                  ffn_w_gate[i, 0], ffn_w_up[i, 0], ffn_w_down[i, 0])
        h = rms_norm(x, norm_g[i, 1])
        mixer = i % N_MIXERS
        if mixer == 0:
            m = forgetting_attention(h, w_ins[i], forget_biases[i])
        elif mixer == 1:
            m = dilated_attention(h, w_ins[i], rel_bias)
        else:
            m = stick_breaking_attention(h, w_ins[i])
        x = x + m @ w_out[i]
        x = x + FFN_HALF * swiglu(rms_norm(x, norm_g[i, 2]),
                                  ffn_w_gate[i, 1], ffn_w_up[i, 1], ffn_w_down[i, 1])
    return rms_norm(x, final_g)
```

```python
import functools
import math

import numpy as np
import jax
import jax.numpy as jnp
from jax import lax
from jax.experimental import pallas as pl
from jax.experimental.pallas import tpu as pltpu

F32 = jnp.float32
BF16 = jnp.bfloat16

N_HEADS = 16
HEAD_DIM = 128
RMS_EPS = 1e-6
FFN_HALF = 0.5
REL_BUCKETS = 32
REL_MAX_DIST = 2048
ATT_BLOCK = 128
DILATED_BRANCHES = ((128, 1), (512, 4), (2048, 16))
NEG_BIG = -1e30

VMEM_LIMIT_BYTES = 56 * 1024 * 1024
NORM_ROWS = 32


def _params(*semantics):
    return pltpu.CompilerParams(dimension_semantics=semantics,
                                vmem_limit_bytes=VMEM_LIMIT_BYTES)


def _rmsnorm_rows(h_ref, x_ref, g_ref):
    rows = x_ref.shape[0]
    g = g_ref[...]

    def body(c, carry):
        r0 = pl.multiple_of(c * NORM_ROWS, NORM_ROWS)
        xc = x_ref[pl.ds(r0, NORM_ROWS), :]
        ms = jnp.mean(xc * xc, axis=-1, keepdims=True)
        h_ref[pl.ds(r0, NORM_ROWS), :] = ((xc * lax.rsqrt(ms + RMS_EPS)) * g).astype(h_ref.dtype)
        return carry

    lax.fori_loop(0, rows // NORM_ROWS, body, 0)


def _ffn_up_kernel(x_ref, g_ref, wg_ref, wu_ref, o_ref, h_ref):
    @pl.when(pl.program_id(1) == 0)
    def _():
        _rmsnorm_rows(h_ref, x_ref, g_ref)

    h = h_ref[...]
    gate = jnp.dot(h, wg_ref[...], preferred_element_type=F32)
    up = jnp.dot(h, wu_ref[...], preferred_element_type=F32)
    o_ref[...] = (gate * jax.nn.sigmoid(gate) * up).astype(o_ref.dtype)


def _ffn_up(x, g, wg, wu, *, tm, tn):
    t, d = x.shape
    f = wg.shape[1]
    return pl.pallas_call(
        _ffn_up_kernel,
        grid=(t // tm, f // tn),
        in_specs=[
            pl.BlockSpec((tm, d), lambda i, j: (i, 0)),
            pl.BlockSpec((1, d), lambda i, j: (0, 0)),
            pl.BlockSpec((d, tn), lambda i, j: (0, j)),
            pl.BlockSpec((d, tn), lambda i, j: (0, j)),
        ],
        out_specs=pl.BlockSpec((tm, tn), lambda i, j: (i, j)),
        out_shape=jax.ShapeDtypeStruct((t, f), BF16),
        scratch_shapes=[pltpu.VMEM((tm, d), BF16)],
        compiler_params=_params("parallel", "arbitrary"),
    )(x, g.reshape(1, d), wg, wu)


def _matmul_residual_kernel(a_ref, w_ref, r_ref, o_ref, *, scale):
    acc = jnp.dot(a_ref[...], w_ref[...], preferred_element_type=F32)
    o_ref[...] = r_ref[...] + scale * acc


def _matmul_residual(a, w, res, *, scale, tm, tn):
    t, k = a.shape
    n = w.shape[1]
    return pl.pallas_call(
        functools.partial(_matmul_residual_kernel, scale=scale),
        grid=(t // tm, n // tn),
        in_specs=[
            pl.BlockSpec((tm, k), lambda i, j: (i, 0)),
            pl.BlockSpec((k, tn), lambda i, j: (0, j)),
            pl.BlockSpec((tm, tn), lambda i, j: (i, j)),
        ],
        out_specs=pl.BlockSpec((tm, tn), lambda i, j: (i, j)),
        out_shape=jax.ShapeDtypeStruct((t, n), F32),
        compiler_params=_params("parallel", "arbitrary"),
    )(a, w, res)


def _qkv_kernel(x_ref, g_ref, w_ref, o_ref, h_ref, *, q_blocks, q_scale):
    j = pl.program_id(1)

    @pl.when(j == 0)
    def _():
        _rmsnorm_rows(h_ref, x_ref, g_ref)

    acc = jnp.dot(h_ref[...], w_ref[...], preferred_element_type=F32)
    scale = jnp.where(j < q_blocks, q_scale, 1.0).astype(F32)
    o_ref[...] = (acc * scale).astype(o_ref.dtype)


def _qkv_forget_kernel(x_ref, g_ref, w_ref, wf_ref, o_ref, f_ref, h_ref, *, q_blocks, q_scale):
    j = pl.program_id(1)

    @pl.when(j == 0)
    def _():
        _rmsnorm_rows(h_ref, x_ref, g_ref)
        f_ref[...] = jnp.dot(h_ref[...], wf_ref[...], preferred_element_type=F32)

    acc = jnp.dot(h_ref[...], w_ref[...], preferred_element_type=F32)
    scale = jnp.where(j < q_blocks, q_scale, 1.0).astype(F32)
    o_ref[...] = (acc * scale).astype(o_ref.dtype)


def _qkv_proj(x, g, w, wf=None, *, out_dtype, tm, tn):
    t, d = x.shape
    n = w.shape[1]
    kw = dict(q_blocks=(n // 3) // tn, q_scale=HEAD_DIM ** -0.5)
    in_specs = [
        pl.BlockSpec((tm, d), lambda i, j: (i, 0)),
        pl.BlockSpec((1, d), lambda i, j: (0, 0)),
        pl.BlockSpec((d, tn), lambda i, j: (0, j)),
    ]
    qkv_spec = pl.BlockSpec((tm, tn), lambda i, j: (i, j))
    qkv_shape = jax.ShapeDtypeStruct((t, n), out_dtype)
    scratch = [pltpu.VMEM((tm, d), BF16)]
    if wf is None:
        return pl.pallas_call(
            functools.partial(_qkv_kernel, **kw),
            grid=(t // tm, n // tn), in_specs=in_specs, out_specs=qkv_spec,
            out_shape=qkv_shape, scratch_shapes=scratch,
            compiler_params=_params("parallel", "arbitrary"),
        )(x, g.reshape(1, d), w)
    nf = wf.shape[1]
    return pl.pallas_call(
        functools.partial(_qkv_forget_kernel, **kw),
        grid=(t // tm, n // tn),
        in_specs=in_specs + [pl.BlockSpec((d, nf), lambda i, j: (0, 0))],
        out_specs=[qkv_spec, pl.BlockSpec((tm, nf), lambda i, j: (i, 0))],
        out_shape=[qkv_shape, jax.ShapeDtypeStruct((t, nf), F32)],
        scratch_shapes=scratch,
        compiler_params=_params("parallel", "arbitrary"),
    )(x, g.reshape(1, d), w, wf)


CUMSUM_ROWS = 256


def _split3(x):
    hi = x.astype(BF16)
    r1 = x - hi.astype(F32)
    mid = r1.astype(BF16)
    lo = (r1 - mid.astype(F32)).astype(BF16)
    return hi, mid, lo


def _forget_cumsum_kernel(f_ref, b_ref, c_ref):
    s = f_ref.shape[0]
    row = lax.broadcasted_iota(jnp.int32, (CUMSUM_ROWS, CUMSUM_ROWS), 0)
    col = lax.broadcasted_iota(jnp.int32, (CUMSUM_ROWS, CUMSUM_ROWS), 1)
    tri = (col <= row).astype(BF16)
    carry = jnp.zeros((1, f_ref.shape[1]), F32)
    for c in range(s // CUMSUM_ROWS):
        z = f_ref[pl.ds(c * CUMSUM_ROWS, CUMSUM_ROWS), :] + b_ref[...]
        log_f = jnp.minimum(z, 0.0) - jnp.log(1.0 + jnp.exp(-jnp.abs(z)))
        hi, mid, lo = _split3(log_f)
        part = (jnp.dot(tri, hi, preferred_element_type=F32)
                + jnp.dot(tri, mid, preferred_element_type=F32)
                + jnp.dot(tri, lo, preferred_element_type=F32))
        out = part + carry
        c_ref[pl.ds(c * CUMSUM_ROWS, CUMSUM_ROWS), :] = out
        carry = out[CUMSUM_ROWS - 1:CUMSUM_ROWS, :]


def _forget_cumsum(f, b, *, batch):
    t, nf = f.shape
    s = t // batch
    return pl.pallas_call(
        _forget_cumsum_kernel,
        grid=(batch,),
        in_specs=[pl.BlockSpec((s, nf), lambda b_: (b_, 0)),
                  pl.BlockSpec((1, nf), lambda b_: (0, 0))],
        out_specs=pl.BlockSpec((s, nf), lambda b_: (b_, 0)),
        out_shape=jax.ShapeDtypeStruct((t, nf), F32),
        compiler_params=_params("parallel"),
    )(f, b)


def _fox_kernel(q_ref, k_ref, v_ref, ccol_ref, crow_ref, o_ref, *, blk):
    h = pl.program_id(1)
    i = pl.program_id(2)
    q = q_ref[...]
    lane = lax.broadcasted_iota(jnp.int32, ccol_ref.shape, 1)
    cq = jnp.sum(jnp.where(lane == h, ccol_ref[...], 0.0), axis=1, keepdims=True)
    row = lax.broadcasted_iota(jnp.int32, (blk, blk), 0)
    col = lax.broadcasted_iota(jnp.int32, (blk, blk), 1)

    def step(j, carry, diagonal):
        m, l, acc = carry
        k0 = pl.multiple_of(j * blk, blk)
        k = k_ref[pl.ds(k0, blk), :]
        v = v_ref[pl.ds(k0, blk), :]
        s = lax.dot_general(q, k, (((1,), (1,)), ((), ())), preferred_element_type=F32)
        s = s + cq - crow_ref[:, pl.ds(k0, blk)]
        if diagonal:
            s = jnp.where(col <= row, s, NEG_BIG)
        m_new = jnp.maximum(m, jnp.max(s, axis=1, keepdims=True))
        p = jnp.exp(s - m_new)
        alpha = jnp.exp(m - m_new)
        l = alpha * l + jnp.sum(p, axis=1, keepdims=True)
        acc = alpha * acc + jnp.dot(p.astype(BF16), v, preferred_element_type=F32)
        return m_new, l, acc

    init = (jnp.full((blk, 1), NEG_BIG, F32), jnp.zeros((blk, 1), F32),
            jnp.zeros((blk, HEAD_DIM), F32))
    carry = lax.fori_loop(0, i, lambda j, c: step(j, c, False), init)
    _, l, acc = step(i, carry, True)
    o_ref[...] = (acc / l).astype(o_ref.dtype)


def _fox_attention(qkv, ccol, crow, *, batch, blk):
    t, n3 = qkv.shape
    s = t // batch
    nh = n3 // (3 * HEAD_DIM)
    qkv3 = qkv.reshape(batch, s, n3)
    ccol3 = ccol.reshape(batch, s, ccol.shape[1])
    out = pl.pallas_call(
        functools.partial(_fox_kernel, blk=blk),
        grid=(batch, nh, s // blk),
        in_specs=[
            pl.BlockSpec((None, blk, HEAD_DIM), lambda b, h, i: (b, i, h)),
            pl.BlockSpec((None, s, HEAD_DIM), lambda b, h, i: (b, 0, nh + h)),
            pl.BlockSpec((None, s, HEAD_DIM), lambda b, h, i: (b, 0, 2 * nh + h)),
            pl.BlockSpec((None, blk, ccol.shape[1]), lambda b, h, i: (b, i, 0)),
            pl.BlockSpec((None, 1, s), lambda b, h, i: (b * nh + h, 0, 0)),
        ],
        out_specs=pl.BlockSpec((None, blk, HEAD_DIM), lambda b, h, i: (b, i, h)),
        out_shape=jax.ShapeDtypeStruct((batch, s, nh * HEAD_DIM), BF16),
        compiler_params=_params("parallel", "parallel", "arbitrary"),
    )(qkv3, qkv3, qkv3, ccol3, crow)
    return out.reshape(t, nh * HEAD_DIM)


def _stick_kernel(q_ref, k_ref, v_ref, o_ref, *, blk):
    i = pl.program_id(2)
    q = q_ref[...]
    row = lax.broadcasted_iota(jnp.int32, (blk, blk), 0)
    col = lax.broadcasted_iota(jnp.int32, (blk, blk), 1)
    jj = lax.broadcasted_iota(jnp.int32, (2 * blk, 2 * blk), 0) % blk
    ss = lax.broadcasted_iota(jnp.int32, (2 * blk, 2 * blk), 1)
    suffix = ((ss >= blk) | (jj > ss)).astype(BF16)

    def step(j, carry, diagonal):
        tail, acc = carry
        k0 = pl.multiple_of(j * blk, blk)
        k = k_ref[pl.ds(k0, blk), :]
        v = v_ref[pl.ds(k0, blk), :]
        z = lax.dot_general(q, k, (((1,), (1,)), ((), ())), preferred_element_type=F32)
        log_beta = jnp.minimum(z, 0.0) - jnp.log(1.0 + jnp.exp(-jnp.abs(z)))
        log_rest = log_beta - z
        if diagonal:
            log_rest = jnp.where(col < row, log_rest, 0.0)
        hi = log_rest.astype(BF16)
        lo = (log_rest - hi.astype(F32)).astype(BF16)
        sums = jnp.dot(jnp.concatenate([hi, lo], axis=1), suffix, preferred_element_type=F32)
        after = tail + sums[:, :blk]
        a = jnp.exp(log_beta + after)
        if diagonal:
            a = jnp.where(col < row, a, 0.0)
        acc = acc + jnp.dot(a.astype(BF16), v, preferred_element_type=F32)
        return tail + sums[:, blk:], acc

    carry = step(i, (jnp.zeros((blk, blk), F32), jnp.zeros((blk, HEAD_DIM), F32)), True)
    _, acc = lax.fori_loop(0, i, lambda n, c: step(i - 1 - n, c, False), carry)
    o_ref[...] = acc.astype(o_ref.dtype)


def _stick_attention(qkv, *, batch, blk):
    t, n3 = qkv.shape
    s = t // batch
    nh = n3 // (3 * HEAD_DIM)
    qkv3 = qkv.reshape(batch, s, n3)
    out = pl.pallas_call(
        functools.partial(_stick_kernel, blk=blk),
        grid=(batch, nh, s // blk),
        in_specs=[
            pl.BlockSpec((None, blk, HEAD_DIM), lambda b, h, i: (b, i, h)),
            pl.BlockSpec((None, s, HEAD_DIM), lambda b, h, i: (b, 0, nh + h)),
            pl.BlockSpec((None, s, HEAD_DIM), lambda b, h, i: (b, 0, 2 * nh + h)),
        ],
        out_specs=pl.BlockSpec((None, blk, HEAD_DIM), lambda b, h, i: (b, i, h)),
        out_shape=jax.ShapeDtypeStruct((batch, s, nh * HEAD_DIM), BF16),
        compiler_params=_params("parallel", "parallel", "arbitrary"),
    )(qkv3, qkv3, qkv3)
    return out.reshape(t, nh * HEAD_DIM)


def _t5_bucket_np(dist):
    max_exact = REL_BUCKETS // 2
    d = np.maximum(dist, 1).astype(np.float32)
    ratio = np.log(d / np.float32(max_exact)) / np.float32(math.log(REL_MAX_DIST / max_exact))
    large = max_exact + (ratio * np.float32(REL_BUCKETS - max_exact)).astype(np.int32)
    large = np.minimum(large, REL_BUCKETS - 1)
    return np.where(dist < max_exact, dist, large).astype(np.int32)


def _dilated_bucket_tables():
    qb = ATT_BLOCK
    qi = np.arange(qb)[:, None]
    kj = np.arange(2 * qb)[None, :]
    off = qb + qi - kj
    tables = []
    for window, dilation in DILATED_BRANCHES:
        span = window // dilation
        band = (off >= 0) & (off <= span)
        bucket = _t5_bucket_np(np.clip(off, 0, None) * dilation)
        tables.append(np.where(band, bucket, -1))
    return np.stack(tables).astype(np.int32)


def _dilated_kernel(rel_ref, idx_ref, q_ref, k_ref, v_ref, o_ref, bias_ref, ob_ref, lse_ref):
    h = pl.program_id(0)
    qb = ATT_BLOCK
    seq = q_ref.shape[0]

    @pl.when(pl.program_id(1) == 0)
    def _():
        for g in range(len(DILATED_BRANCHES)):
            idx = idx_ref[g]
            bias = jnp.full(idx.shape, NEG_BIG, F32)
            for b in range(REL_BUCKETS):
                bias = jnp.where(idx == b, rel_ref[h, b], bias)
            bias_ref[g] = bias

    for g, (_, dil) in enumerate(DILATED_BRANCHES):
        length = seq // dil
        nb = length // qb

        def rows(start, dil=dil):
            if dil == 1:
                return pl.ds(pl.multiple_of(start, qb), qb)
            return pl.ds(start, qb, stride=dil)

        def block(c, carry, g=g, dil=dil, nb=nb, rows=rows):
            r = c // nb
            n = c % nb
            start = n * (qb * dil) + r
            cur = rows(start)
            q = q_ref[cur, :].astype(BF16)
            k = k_ref[cur, :].astype(BF16)
            v = v_ref[cur, :].astype(BF16)
            s_cur = lax.dot_general(q, k, (((1,), (1,)), ((), ())), preferred_element_type=F32)
            s_cur = s_cur + bias_ref[g, :, qb:]
            m = jnp.max(s_cur, axis=1, keepdims=True)
            if nb > 1:
                prev = rows(jnp.maximum(start - qb * dil, r))
                kp = k_ref[prev, :].astype(BF16)
                vp = v_ref[prev, :].astype(BF16)
                s_prev = lax.dot_general(q, kp, (((1,), (1,)), ((), ())), preferred_element_type=F32)
                s_prev = jnp.where(n > 0, s_prev + bias_ref[g, :, :qb], NEG_BIG)
                m = jnp.maximum(m, jnp.max(s_prev, axis=1, keepdims=True))
                e_prev = jnp.exp(s_prev - m)
            e_cur = jnp.exp(s_cur - m)
            den = jnp.sum(e_cur, axis=1, keepdims=True)
            acc = jnp.dot(e_cur.astype(BF16), v, preferred_element_type=F32)
            if nb > 1:
                den = den + jnp.sum(e_prev, axis=1, keepdims=True)
                acc = acc + jnp.dot(e_prev.astype(BF16), vp, preferred_element_type=F32)
            ob_ref[g, cur, :] = acc / den
            lse_ref[g, cur, :] = jnp.broadcast_to(m + jnp.log(den), (qb, HEAD_DIM))
            return carry

        lax.fori_loop(0, dil * nb, block, 0)

    def merge(c, carry):
        rs = pl.ds(pl.multiple_of(c * qb, qb), qb)
        lse = [lse_ref[g, rs, :] for g in range(len(DILATED_BRANCHES))]
        top = functools.reduce(jnp.maximum, lse)
        w = [jnp.exp(x - top) for x in lse]
        num = sum(w[g] * ob_ref[g, rs, :] for g in range(len(DILATED_BRANCHES)))
        o_ref[rs, :] = (num / sum(w)).astype(o_ref.dtype)
        return carry

    lax.fori_loop(0, seq // qb, merge, 0)


def _dilated_attention(qkv, rel_bias, *, batch):
    t, n3 = qkv.shape
    s = t // batch
    nh = n3 // (3 * HEAD_DIM)
    ng = len(DILATED_BRANCHES)
    qkv3 = qkv.reshape(batch, s, n3)
    idx = jnp.asarray(_dilated_bucket_tables())
    out = pl.pallas_call(
        _dilated_kernel,
        grid=(nh, batch),
        in_specs=[
            pl.BlockSpec(memory_space=pltpu.SMEM),
            pl.BlockSpec(idx.shape, lambda h, b: (0, 0, 0)),
            pl.BlockSpec((None, s, HEAD_DIM), lambda h, b: (b, 0, h)),
            pl.BlockSpec((None, s, HEAD_DIM), lambda h, b: (b, 0, nh + h)),
            pl.BlockSpec((None, s, HEAD_DIM), lambda h, b: (b, 0, 2 * nh + h)),
        ],
        out_specs=pl.BlockSpec((None, s, HEAD_DIM), lambda h, b: (b, 0, h)),
        out_shape=jax.ShapeDtypeStruct((batch, s, nh * HEAD_DIM), BF16),
        scratch_shapes=[pltpu.VMEM((ng, ATT_BLOCK, 2 * ATT_BLOCK), F32),
                        pltpu.VMEM((ng, s, HEAD_DIM), F32),
                        pltpu.VMEM((ng, s, HEAD_DIM), F32)],
        compiler_params=_params("arbitrary", "arbitrary"),
    )(rel_bias, idx, qkv3, qkv3, qkv3)
    return out.reshape(t, nh * HEAD_DIM)


def _final_norm_kernel(x_ref, g_ref, o_ref):
    _rmsnorm_rows(o_ref, x_ref, g_ref)


def _final_norm(x, g, *, tm):
    t, d = x.shape
    return pl.pallas_call(
        _final_norm_kernel,
        grid=(t // tm,),
        in_specs=[pl.BlockSpec((tm, d), lambda i: (i, 0)),
                  pl.BlockSpec((1, d), lambda i: (0, 0))],
        out_specs=pl.BlockSpec((tm, d), lambda i: (i, 0)),
        out_shape=jax.ShapeDtypeStruct((t, d), F32),
        compiler_params=_params("parallel"),
    )(x, g.reshape(1, d))


def _tiles(t):
    tm = min(1024, t)
    return dict(tm=tm, tn=512)


def _ffn(x, g, wg, wu, wd):
    tiles = _tiles(x.shape[0])
    a = _ffn_up(x, g, wg, wu, **tiles)
    return _matmul_residual(a, wd, x, scale=FFN_HALF, **tiles)


def _split_forget(w_in, b_f):
    d3 = w_in.shape[1] - N_HEADS
    wf = jnp.pad(w_in[:, d3:], ((0, 0), (0, HEAD_DIM - N_HEADS))).astype(BF16)
    bf = jnp.pad(b_f, (0, HEAD_DIM - N_HEADS)).reshape(1, HEAD_DIM).astype(F32)
    return w_in[:, :d3].astype(BF16), wf, bf


def kernel(x, norm_g, ffn_w_gate, ffn_w_up, ffn_w_down, w_in_0, b_f_0, w_in_1, w_in_2,
           w_in_3, b_f_3, w_out, rel_bias, final_g):
    batch, seq, d = x.shape
    t = batch * seq
    depth = norm_g.shape[0]
    tiles = _tiles(t)
    wg = ffn_w_gate.astype(BF16)
    wu = ffn_w_up.astype(BF16)
    wd = ffn_w_down.astype(BF16)
    wo = w_out.astype(BF16)
    w_ins = (w_in_0, w_in_1, w_in_2, w_in_3)
    forget_biases = {0: b_f_0, 3: b_f_3}

    xt = x.reshape(t, d)
    for i in range(depth):
        xt = _ffn(xt, norm_g[i, 0], wg[i, 0], wu[i, 0], wd[i, 0])
        mixer = i % 3
        if mixer == 0:
            w_qkv, wf, bf = _split_forget(w_ins[i], forget_biases[i])
            qkv, f = _qkv_proj(xt, norm_g[i, 1], w_qkv, wf, out_dtype=BF16, **tiles)
            ccol = _forget_cumsum(f, bf, batch=batch)
            crow = ccol.reshape(batch, seq, HEAD_DIM)[:, :, :N_HEADS]
            crow = jnp.transpose(crow, (0, 2, 1)).reshape(batch * N_HEADS, 1, seq)
            o = _fox_attention(qkv, ccol, crow, batch=batch, blk=256)
        elif mixer == 1:
            qkv = _qkv_proj(xt, norm_g[i, 1], w_ins[i].astype(BF16), out_dtype=F32, **tiles)
            o = _dilated_attention(qkv, rel_bias.astype(F32), batch=batch)
        else:
            qkv = _qkv_proj(xt, norm_g[i, 1], w_ins[i].astype(BF16), out_dtype=BF16, **tiles)
            o = _stick_attention(qkv, batch=batch, blk=128)
        xt = _matmul_residual(o, wo[i], xt, scale=1.0, **tiles)
        xt = _ffn(xt, norm_g[i, 2], wg[i, 1], wu[i, 1], wd[i, 1])
    return _final_norm(xt, final_g, tm=tiles["tm"]).reshape(batch, seq, d)
```

```python
import functools
import math

import numpy as np
import jax
import jax.numpy as jnp
from jax import lax
from jax.experimental import pallas as pl
from jax.experimental.pallas import tpu as pltpu

F32 = jnp.float32
BF16 = jnp.bfloat16

N_HEADS = 16
HEAD_DIM = 128
RMS_EPS = 1e-6
FFN_HALF = 0.5
REL_BUCKETS = 32
REL_MAX_DIST = 2048
ATT_BLOCK = 128
BLOCKS_PER_STEP = 4
DILATED_BRANCHES = ((128, 1), (512, 4), (2048, 16))
NEG_BIG = -1e30

VMEM_LIMIT_BYTES = 56 * 1024 * 1024
NORM_ROWS = 32


def _params(*semantics):
    return pltpu.CompilerParams(dimension_semantics=semantics,
                                vmem_limit_bytes=VMEM_LIMIT_BYTES)


def _rmsnorm_rows(h_ref, x_ref, g_ref):
    rows = x_ref.shape[0]
    g = g_ref[...]

    def body(c, carry):
        r0 = pl.multiple_of(c * NORM_ROWS, NORM_ROWS)
        xc = x_ref[pl.ds(r0, NORM_ROWS), :]
        ms = jnp.mean(xc * xc, axis=-1, keepdims=True)
        h_ref[pl.ds(r0, NORM_ROWS), :] = ((xc * lax.rsqrt(ms + RMS_EPS)) * g).astype(h_ref.dtype)
        return carry

    lax.fori_loop(0, rows // NORM_ROWS, body, 0)


def _ffn_up_kernel(x_ref, g_ref, wg_ref, wu_ref, o_ref, h_ref):
    @pl.when(pl.program_id(1) == 0)
    def _():
        _rmsnorm_rows(h_ref, x_ref, g_ref)

    h = h_ref[...]
    gate = jnp.dot(h, wg_ref[...], preferred_element_type=F32)
    up = jnp.dot(h, wu_ref[...], preferred_element_type=F32)
    o_ref[...] = (gate * jax.nn.sigmoid(gate) * up).astype(o_ref.dtype)


def _ffn_up(x, g, wg, wu, *, tm, tn):
    t, d = x.shape
    f = wg.shape[1]
    return pl.pallas_call(
        _ffn_up_kernel,
        grid=(t // tm, f // tn),
        in_specs=[
            pl.BlockSpec((tm, d), lambda i, j: (i, 0)),
            pl.BlockSpec((1, d), lambda i, j: (0, 0)),
            pl.BlockSpec((d, tn), lambda i, j: (0, j)),
            pl.BlockSpec((d, tn), lambda i, j: (0, j)),
        ],
        out_specs=pl.BlockSpec((tm, tn), lambda i, j: (i, j)),
        out_shape=jax.ShapeDtypeStruct((t, f), BF16),
        scratch_shapes=[pltpu.VMEM((tm, d), BF16)],
        compiler_params=_params("parallel", "arbitrary"),
    )(x, g.reshape(1, d), wg, wu)


def _matmul_residual_kernel(a_ref, w_ref, r_ref, o_ref, *, scale):
    acc = jnp.dot(a_ref[...], w_ref[...], preferred_element_type=F32)
    o_ref[...] = r_ref[...] + scale * acc


def _matmul_residual(a, w, res, *, scale, tm, tn):
    t, k = a.shape
    n = w.shape[1]
    return pl.pallas_call(
        functools.partial(_matmul_residual_kernel, scale=scale),
        grid=(t // tm, n // tn),
        in_specs=[
            pl.BlockSpec((tm, k), lambda i, j: (i, 0)),
            pl.BlockSpec((k, tn), lambda i, j: (0, j)),
            pl.BlockSpec((tm, tn), lambda i, j: (i, j)),
        ],
        out_specs=pl.BlockSpec((tm, tn), lambda i, j: (i, j)),
        out_shape=jax.ShapeDtypeStruct((t, n), F32),
        compiler_params=_params("parallel", "arbitrary"),
    )(a, w, res)


def _qkv_kernel(x_ref, g_ref, w_ref, o_ref, h_ref, *, q_blocks, q_scale):
    j = pl.program_id(1)

    @pl.when(j == 0)
    def _():
        _rmsnorm_rows(h_ref, x_ref, g_ref)

    acc = jnp.dot(h_ref[...], w_ref[...], preferred_element_type=F32)
    scale = jnp.where(j < q_blocks, q_scale, 1.0).astype(F32)
    o_ref[...] = (acc * scale).astype(o_ref.dtype)


def _qkv_forget_kernel(x_ref, g_ref, w_ref, wf_ref, o_ref, f_ref, h_ref, *, q_blocks, q_scale):
    j = pl.program_id(1)

    @pl.when(j == 0)
    def _():
        _rmsnorm_rows(h_ref, x_ref, g_ref)
        f_ref[...] = jnp.dot(h_ref[...], wf_ref[...], preferred_element_type=F32)

    acc = jnp.dot(h_ref[...], w_ref[...], preferred_element_type=F32)
    scale = jnp.where(j < q_blocks, q_scale, 1.0).astype(F32)
    o_ref[...] = (acc * scale).astype(o_ref.dtype)


def _qkv_proj(x, g, w, wf=None, *, out_dtype, tm, tn):
    t, d = x.shape
    n = w.shape[1]
    kw = dict(q_blocks=(n // 3) // tn, q_scale=HEAD_DIM ** -0.5)
    in_specs = [
        pl.BlockSpec((tm, d), lambda i, j: (i, 0)),
        pl.BlockSpec((1, d), lambda i, j: (0, 0)),
        pl.BlockSpec((d, tn), lambda i, j: (0, j)),
    ]
    qkv_spec = pl.BlockSpec((tm, tn), lambda i, j: (i, j))
    qkv_shape = jax.ShapeDtypeStruct((t, n), out_dtype)
    scratch = [pltpu.VMEM((tm, d), BF16)]
    if wf is None:
        return pl.pallas_call(
            functools.partial(_qkv_kernel, **kw),
            grid=(t // tm, n // tn), in_specs=in_specs, out_specs=qkv_spec,
            out_shape=qkv_shape, scratch_shapes=scratch,
            compiler_params=_params("parallel", "arbitrary"),
        )(x, g.reshape(1, d), w)
    nf = wf.shape[1]
    return pl.pallas_call(
        functools.partial(_qkv_forget_kernel, **kw),
        grid=(t // tm, n // tn),
        in_specs=in_specs + [pl.BlockSpec((d, nf), lambda i, j: (0, 0))],
        out_specs=[qkv_spec, pl.BlockSpec((tm, nf), lambda i, j: (i, 0))],
        out_shape=[qkv_shape, jax.ShapeDtypeStruct((t, nf), F32)],
        scratch_shapes=scratch,
        compiler_params=_params("parallel", "arbitrary"),
    )(x, g.reshape(1, d), w, wf)


CUMSUM_ROWS = 256


def _split3(x):
    hi = x.astype(BF16)
    r1 = x - hi.astype(F32)
    mid = r1.astype(BF16)
    lo = (r1 - mid.astype(F32)).astype(BF16)
    return hi, mid, lo


def _forget_cumsum_kernel(f_ref, b_ref, c_ref):
    s = f_ref.shape[0]
    row = lax.broadcasted_iota(jnp.int32, (CUMSUM_ROWS, CUMSUM_ROWS), 0)
    col = lax.broadcasted_iota(jnp.int32, (CUMSUM_ROWS, CUMSUM_ROWS), 1)
    tri = (col <= row).astype(BF16)
    carry = jnp.zeros((1, f_ref.shape[1]), F32)
    for c in range(s // CUMSUM_ROWS):
        z = f_ref[pl.ds(c * CUMSUM_ROWS, CUMSUM_ROWS), :] + b_ref[...]
        log_f = jnp.minimum(z, 0.0) - jnp.log(1.0 + jnp.exp(-jnp.abs(z)))
        hi, mid, lo = _split3(log_f)
        part = (jnp.dot(tri, hi, preferred_element_type=F32)
                + jnp.dot(tri, mid, preferred_element_type=F32)
                + jnp.dot(tri, lo, preferred_element_type=F32))
        out = part + carry
        c_ref[pl.ds(c * CUMSUM_ROWS, CUMSUM_ROWS), :] = out
        carry = out[CUMSUM_ROWS - 1:CUMSUM_ROWS, :]


def _forget_cumsum(f, b, *, batch):
    t, nf = f.shape
    s = t // batch
    return pl.pallas_call(
        _forget_cumsum_kernel,
        grid=(batch,),
        in_specs=[pl.BlockSpec((s, nf), lambda b_: (b_, 0)),
                  pl.BlockSpec((1, nf), lambda b_: (0, 0))],
        out_specs=pl.BlockSpec((s, nf), lambda b_: (b_, 0)),
        out_shape=jax.ShapeDtypeStruct((t, nf), F32),
        compiler_params=_params("parallel"),
    )(f, b)


def _fox_kernel(q_ref, k_ref, v_ref, ccol_ref, crow_ref, o_ref, *, tq, tk):
    h = pl.program_id(1)
    i = pl.program_id(2)
    q = q_ref[...]
    lane = lax.broadcasted_iota(jnp.int32, ccol_ref.shape, 1)
    cq = jnp.sum(jnp.where(lane == h, ccol_ref[...], 0.0), axis=1, keepdims=True)
    row = lax.broadcasted_iota(jnp.int32, (tq, tk), 0)
    col = lax.broadcasted_iota(jnp.int32, (tq, tk), 1)
    n_diag = tq // tk

    def step(j, carry, diag):
        m, l, acc = carry
        k0 = pl.multiple_of(j * tk, tk)
        k = k_ref[pl.ds(k0, tk), :]
        v = v_ref[pl.ds(k0, tk), :]
        s = lax.dot_general(q, k, (((1,), (1,)), ((), ())), preferred_element_type=F32)
        s = s + cq - crow_ref[:, pl.ds(k0, tk)]
        if diag is not None:
            s = jnp.where(col + diag * tk <= row, s, NEG_BIG)
        m_new = jnp.maximum(m, jnp.max(s, axis=1, keepdims=True))
        p = jnp.exp(s - m_new)
        alpha = jnp.exp(m - m_new)
        l = alpha * l + jnp.sum(p, axis=1, keepdims=True)
        acc = alpha * acc + jnp.dot(p.astype(BF16), v, preferred_element_type=F32)
        return m_new, l, acc

    carry = (jnp.full((tq, 1), NEG_BIG, F32), jnp.zeros((tq, 1), F32),
             jnp.zeros((tq, HEAD_DIM), F32))
    carry = lax.fori_loop(0, i * n_diag, lambda j, c: step(j, c, None), carry)
    for d in range(n_diag):
        carry = step(i * n_diag + d, carry, d)
    _, l, acc = carry
    o_ref[...] = (acc / l).astype(o_ref.dtype)


def _fox_attention(qkv, ccol, crow, *, batch, tq, tk):
    t, n3 = qkv.shape
    s = t // batch
    nh = n3 // (3 * HEAD_DIM)
    qkv3 = qkv.reshape(batch, s, n3)
    ccol3 = ccol.reshape(batch, s, ccol.shape[1])
    out = pl.pallas_call(
        functools.partial(_fox_kernel, tq=tq, tk=tk),
        grid=(batch, nh, s // tq),
        in_specs=[
            pl.BlockSpec((None, tq, HEAD_DIM), lambda b, h, i: (b, i, h)),
            pl.BlockSpec((None, s, HEAD_DIM), lambda b, h, i: (b, 0, nh + h)),
            pl.BlockSpec((None, s, HEAD_DIM), lambda b, h, i: (b, 0, 2 * nh + h)),
            pl.BlockSpec((None, tq, ccol.shape[1]), lambda b, h, i: (b, i, 0)),
            pl.BlockSpec((None, 1, s), lambda b, h, i: (b * nh + h, 0, 0)),
        ],
        out_specs=pl.BlockSpec((None, tq, HEAD_DIM), lambda b, h, i: (b, i, h)),
        out_shape=jax.ShapeDtypeStruct((batch, s, nh * HEAD_DIM), BF16),
        compiler_params=_params("parallel", "parallel", "arbitrary"),
    )(qkv3, qkv3, qkv3, ccol3, crow)
    return out.reshape(t, nh * HEAD_DIM)


SUFFIX_BLOCK = 128


def _stick_kernel(q_ref, k_ref, v_ref, o_ref, *, tq, tk):
    i = pl.program_id(2)
    sb = SUFFIX_BLOCK
    q = q_ref[...]
    row = lax.broadcasted_iota(jnp.int32, (tq, tk), 0)
    col = lax.broadcasted_iota(jnp.int32, (tq, tk), 1)
    jj = lax.broadcasted_iota(jnp.int32, (2 * sb, 2 * sb), 0) % sb
    ss = lax.broadcasted_iota(jnp.int32, (2 * sb, 2 * sb), 1)
    suffix = ((ss >= sb) | (jj > ss)).astype(BF16)
    n_diag = tq // tk

    def step(j, carry, diag):
        tail, acc = carry
        k0 = pl.multiple_of(j * tk, tk)
        k = k_ref[pl.ds(k0, tk), :]
        v = v_ref[pl.ds(k0, tk), :]
        z = lax.dot_general(q, k, (((1,), (1,)), ((), ())), preferred_element_type=F32)
        log_beta = jnp.minimum(z, 0.0) - jnp.log(1.0 + jnp.exp(-jnp.abs(z)))
        log_rest = log_beta - z
        if diag is not None:
            causal = col + diag * tk < row
            log_rest = jnp.where(causal, log_rest, 0.0)
        hi = log_rest.astype(BF16)
        lo = (log_rest - hi.astype(F32)).astype(BF16)
        after = [None] * (tk // sb)
        for u in reversed(range(tk // sb)):
            cols = slice(u * sb, (u + 1) * sb)
            sums = jnp.dot(jnp.concatenate([hi[:, cols], lo[:, cols]], axis=1), suffix,
                           preferred_element_type=F32)
            after[u] = tail + sums[:, :sb]
            tail = tail + sums[:, sb:]
        a = jnp.exp(log_beta + jnp.concatenate(after, axis=1))
        if diag is not None:
            a = jnp.where(causal, a, 0.0)
        acc = acc + jnp.dot(a.astype(BF16), v, preferred_element_type=F32)
        return tail, acc

    carry = (jnp.zeros((tq, sb), F32), jnp.zeros((tq, HEAD_DIM), F32))
    for d in reversed(range(n_diag)):
        carry = step(i * n_diag + d, carry, d)
    _, acc = lax.fori_loop(0, i * n_diag, lambda n, c: step(i * n_diag - 1 - n, c, None), carry)
    o_ref[...] = acc.astype(o_ref.dtype)


def _stick_attention(qkv, *, batch, tq, tk):
    t, n3 = qkv.shape
    s = t // batch
    nh = n3 // (3 * HEAD_DIM)
    qkv3 = qkv.reshape(batch, s, n3)
    out = pl.pallas_call(
        functools.partial(_stick_kernel, tq=tq, tk=tk),
        grid=(batch, nh, s // tq),
        in_specs=[
            pl.BlockSpec((None, tq, HEAD_DIM), lambda b, h, i: (b, i, h)),
            pl.BlockSpec((None, s, HEAD_DIM), lambda b, h, i: (b, 0, nh + h)),
            pl.BlockSpec((None, s, HEAD_DIM), lambda b, h, i: (b, 0, 2 * nh + h)),
        ],
        out_specs=pl.BlockSpec((None, tq, HEAD_DIM), lambda b, h, i: (b, i, h)),
        out_shape=jax.ShapeDtypeStruct((batch, s, nh * HEAD_DIM), BF16),
        compiler_params=_params("parallel", "parallel", "arbitrary"),
    )(qkv3, qkv3, qkv3)
    return out.reshape(t, nh * HEAD_DIM)


def _t5_bucket_np(dist):
    max_exact = REL_BUCKETS // 2
    d = np.maximum(dist, 1).astype(np.float32)
    ratio = np.log(d / np.float32(max_exact)) / np.float32(math.log(REL_MAX_DIST / max_exact))
    large = max_exact + (ratio * np.float32(REL_BUCKETS - max_exact)).astype(np.int32)
    large = np.minimum(large, REL_BUCKETS - 1)
    return np.where(dist < max_exact, dist, large).astype(np.int32)


def _dilated_bucket_tables():
    qb = ATT_BLOCK
    qi = np.arange(qb)[:, None]
    kj = np.arange(2 * qb)[None, :]
    off = qb + qi - kj
    tables = []
    for window, dilation in DILATED_BRANCHES:
        span = window // dilation
        band = (off >= 0) & (off <= span)
        bucket = _t5_bucket_np(np.clip(off, 0, None) * dilation)
        tables.append(np.where(band, bucket, -1))
    return np.stack(tables).astype(np.int32)


def _dilated_kernel(rel_ref, idx_ref, q_ref, k_ref, v_ref, o_ref, bias_ref, ob_ref, lse_ref):
    h = pl.program_id(0)
    qb = ATT_BLOCK
    seq = q_ref.shape[0]

    @pl.when(pl.program_id(1) == 0)
    def _():
        for g in range(len(DILATED_BRANCHES)):
            idx = idx_ref[g]
            bias = jnp.full(idx.shape, NEG_BIG, F32)
            for b in range(REL_BUCKETS):
                bias = jnp.where(idx == b, rel_ref[h, b], bias)
            bias_ref[g] = bias

    for g, (_, dil) in enumerate(DILATED_BRANCHES):
        length = seq // dil
        nb = length // qb

        def rows(start, dil=dil):
            if dil == 1:
                return pl.ds(pl.multiple_of(start, qb), qb)
            return pl.ds(start, qb, stride=dil)

        def block(c, g=g, dil=dil, nb=nb, rows=rows):
            r = c // nb
            n = c % nb
            start = n * (qb * dil) + r
            cur = rows(start)
            q = q_ref[cur, :].astype(BF16)
            k = k_ref[cur, :].astype(BF16)
            v = v_ref[cur, :].astype(BF16)
            s_cur = lax.dot_general(q, k, (((1,), (1,)), ((), ())), preferred_element_type=F32)
            s_cur = s_cur + bias_ref[g, :, qb:]
            m = jnp.max(s_cur, axis=1, keepdims=True)
            if nb > 1:
                prev = rows(jnp.maximum(start - qb * dil, r))
                kp = k_ref[prev, :].astype(BF16)
                vp = v_ref[prev, :].astype(BF16)
                s_prev = lax.dot_general(q, kp, (((1,), (1,)), ((), ())), preferred_element_type=F32)
                s_prev = jnp.where(n > 0, s_prev + bias_ref[g, :, :qb], NEG_BIG)
                m = jnp.maximum(m, jnp.max(s_prev, axis=1, keepdims=True))
                e_prev = jnp.exp(s_prev - m)
            e_cur = jnp.exp(s_cur - m)
            den = jnp.sum(e_cur, axis=1, keepdims=True)
            acc = jnp.dot(e_cur.astype(BF16), v, preferred_element_type=F32)
            if nb > 1:
                den = den + jnp.sum(e_prev, axis=1, keepdims=True)
                acc = acc + jnp.dot(e_prev.astype(BF16), vp, preferred_element_type=F32)
            ob_ref[g, cur, :] = acc / den
            lse_ref[g, cur, :] = jnp.broadcast_to(m + jnp.log(den), (qb, HEAD_DIM))

        def group(cg, carry, block=block):
            for u in range(BLOCKS_PER_STEP):
                block(cg * BLOCKS_PER_STEP + u)
            return carry

        lax.fori_loop(0, (dil * nb) // BLOCKS_PER_STEP, group, 0)

    def merge(c, carry):
        rs = pl.ds(pl.multiple_of(c * qb, qb), qb)
        lse = [lse_ref[g, rs, :] for g in range(len(DILATED_BRANCHES))]
        top = functools.reduce(jnp.maximum, lse)
        w = [jnp.exp(x - top) for x in lse]
        num = sum(w[g] * ob_ref[g, rs, :] for g in range(len(DILATED_BRANCHES)))
        o_ref[rs, :] = (num / sum(w)).astype(o_ref.dtype)
        return carry

    lax.fori_loop(0, seq // qb, merge, 0)


def _dilated_attention(qkv, rel_bias, *, batch):
    t, n3 = qkv.shape
    s = t // batch
    nh = n3 // (3 * HEAD_DIM)
    ng = len(DILATED_BRANCHES)
    qkv3 = qkv.reshape(batch, s, n3)
    idx = jnp.asarray(_dilated_bucket_tables())
    out = pl.pallas_call(
        _dilated_kernel,
        grid=(nh, batch),
        in_specs=[
            pl.BlockSpec(memory_space=pltpu.SMEM),
            pl.BlockSpec(idx.shape, lambda h, b: (0, 0, 0)),
            pl.BlockSpec((None, s, HEAD_DIM), lambda h, b: (b, 0, h)),
            pl.BlockSpec((None, s, HEAD_DIM), lambda h, b: (b, 0, nh + h)),
            pl.BlockSpec((None, s, HEAD_DIM), lambda h, b: (b, 0, 2 * nh + h)),
        ],
        out_specs=pl.BlockSpec((None, s, HEAD_DIM), lambda h, b: (b, 0, h)),
        out_shape=jax.ShapeDtypeStruct((batch, s, nh * HEAD_DIM), BF16),
        scratch_shapes=[pltpu.VMEM((ng, ATT_BLOCK, 2 * ATT_BLOCK), F32),
                        pltpu.VMEM((ng, s, HEAD_DIM), F32),
                        pltpu.VMEM((ng, s, HEAD_DIM), F32)],
        compiler_params=_params("arbitrary", "arbitrary"),
    )(rel_bias, idx, qkv3, qkv3, qkv3)
    return out.reshape(t, nh * HEAD_DIM)


def _final_norm_kernel(x_ref, g_ref, o_ref):
    _rmsnorm_rows(o_ref, x_ref, g_ref)


def _final_norm(x, g, *, tm):
    t, d = x.shape
    return pl.pallas_call(
        _final_norm_kernel,
        grid=(t // tm,),
        in_specs=[pl.BlockSpec((tm, d), lambda i: (i, 0)),
                  pl.BlockSpec((1, d), lambda i: (0, 0))],
        out_specs=pl.BlockSpec((tm, d), lambda i: (i, 0)),
        out_shape=jax.ShapeDtypeStruct((t, d), F32),
        compiler_params=_params("parallel"),
    )(x, g.reshape(1, d))


def _tiles(t):
    tm = min(1024, t)
    return dict(tm=tm, tn=512)


def _ffn(x, g, wg, wu, wd):
    tiles = _tiles(x.shape[0])
    a = _ffn_up(x, g, wg, wu, **tiles)
    return _matmul_residual(a, wd, x, scale=FFN_HALF, **tiles)


def _split_forget(w_in, b_f):
    d3 = w_in.shape[1] - N_HEADS
    wf = jnp.pad(w_in[:, d3:], ((0, 0), (0, HEAD_DIM - N_HEADS))).astype(BF16)
    bf = jnp.pad(b_f, (0, HEAD_DIM - N_HEADS)).reshape(1, HEAD_DIM).astype(F32)
    return w_in[:, :d3].astype(BF16), wf, bf


def kernel(x, norm_g, ffn_w_gate, ffn_w_up, ffn_w_down, w_in_0, b_f_0, w_in_1, w_in_2,
           w_in_3, b_f_3, w_out, rel_bias, final_g):
    batch, seq, d = x.shape
    t = batch * seq
    depth = norm_g.shape[0]
    tiles = _tiles(t)
    wg = ffn_w_gate.astype(BF16)
    wu = ffn_w_up.astype(BF16)
    wd = ffn_w_down.astype(BF16)
    wo = w_out.astype(BF16)
    w_ins = (w_in_0, w_in_1, w_in_2, w_in_3)
    forget_biases = {0: b_f_0, 3: b_f_3}

    xt = x.reshape(t, d)
    for i in range(depth):
        xt = _ffn(xt, norm_g[i, 0], wg[i, 0], wu[i, 0], wd[i, 0])
        mixer = i % 3
        if mixer == 0:
            w_qkv, wf, bf = _split_forget(w_ins[i], forget_biases[i])
            qkv, f = _qkv_proj(xt, norm_g[i, 1], w_qkv, wf, out_dtype=BF16, **tiles)
            ccol = _forget_cumsum(f, bf, batch=batch)
            crow = ccol.reshape(batch, seq, HEAD_DIM)[:, :, :N_HEADS]
            crow = jnp.transpose(crow, (0, 2, 1)).reshape(batch * N_HEADS, 1, seq)
            o = _fox_attention(qkv, ccol, crow, batch=batch, tq=512, tk=256)
        elif mixer == 1:
            qkv = _qkv_proj(xt, norm_g[i, 1], w_ins[i].astype(BF16), out_dtype=F32, **tiles)
            o = _dilated_attention(qkv, rel_bias.astype(F32), batch=batch)
        else:
            qkv = _qkv_proj(xt, norm_g[i, 1], w_ins[i].astype(BF16), out_dtype=BF16, **tiles)
            o = _stick_attention(qkv, batch=batch, tq=512, tk=256)
        xt = _matmul_residual(o, wo[i], xt, scale=1.0, **tiles)
        xt = _ffn(xt, norm_g[i, 2], wg[i, 1], wu[i, 1], wd[i, 1])
    return _final_norm(xt, final_g, tm=tiles["tm"]).reshape(batch, seq, d)
```

```python
import functools
import math

import numpy as np
import jax
import jax.numpy as jnp
from jax import lax
from jax.experimental import pallas as pl
from jax.experimental.pallas import tpu as pltpu

F32 = jnp.float32
BF16 = jnp.bfloat16

N_HEADS = 16
HEAD_DIM = 128
RMS_EPS = 1e-6
FFN_HALF = 0.5
REL_BUCKETS = 32
REL_MAX_DIST = 2048
ATT_BLOCK = 128
BLOCKS_PER_STEP = 4
DILATED_BRANCHES = ((128, 1), (512, 4), (2048, 16))
NEG_BIG = -1e30

VMEM_LIMIT_BYTES = 56 * 1024 * 1024
NORM_ROWS = 32
NORM_UNROLL = 4


def _params(*semantics):
    return pltpu.CompilerParams(dimension_semantics=semantics,
                                vmem_limit_bytes=VMEM_LIMIT_BYTES)


def _rmsnorm_rows(h_ref, x_ref, g_ref):
    rows = x_ref.shape[0]
    g = g_ref[...]

    def body(c, carry):
        r0 = pl.multiple_of(c * NORM_ROWS, NORM_ROWS)
        xc = x_ref[pl.ds(r0, NORM_ROWS), :]
        ms = jnp.mean(xc * xc, axis=-1, keepdims=True)
        h_ref[pl.ds(r0, NORM_ROWS), :] = ((xc * lax.rsqrt(ms + RMS_EPS)) * g).astype(h_ref.dtype)
        return carry

    lax.fori_loop(0, rows // NORM_ROWS, body, 0, unroll=NORM_UNROLL)


def _ffn_up_kernel(x_ref, g_ref, wg_ref, wu_ref, o_ref, h_ref):
    @pl.when(pl.program_id(1) == 0)
    def _():
        _rmsnorm_rows(h_ref, x_ref, g_ref)

    h = h_ref[...]
    gate = jnp.dot(h, wg_ref[...], preferred_element_type=F32)
    up = jnp.dot(h, wu_ref[...], preferred_element_type=F32)
    o_ref[...] = (gate * jax.nn.sigmoid(gate) * up).astype(o_ref.dtype)


def _ffn_up(x, g, wg, wu, *, tm, tn):
    t, d = x.shape
    f = wg.shape[1]
    return pl.pallas_call(
        _ffn_up_kernel,
        grid=(t // tm, f // tn),
        in_specs=[
            pl.BlockSpec((tm, d), lambda i, j: (i, 0)),
            pl.BlockSpec((1, d), lambda i, j: (0, 0)),
            pl.BlockSpec((d, tn), lambda i, j: (0, j)),
            pl.BlockSpec((d, tn), lambda i, j: (0, j)),
        ],
        out_specs=pl.BlockSpec((tm, tn), lambda i, j: (i, j)),
        out_shape=jax.ShapeDtypeStruct((t, f), BF16),
        scratch_shapes=[pltpu.VMEM((tm, d), BF16)],
        compiler_params=_params("parallel", "arbitrary"),
    )(x, g.reshape(1, d), wg, wu)


def _matmul_residual_kernel(a_ref, w_ref, r_ref, o_ref, *, scale):
    acc = jnp.dot(a_ref[...], w_ref[...], preferred_element_type=F32)
    o_ref[...] = r_ref[...] + scale * acc


def _matmul_residual(a, w, res, *, scale, tm, tn):
    t, k = a.shape
    n = w.shape[1]
    return pl.pallas_call(
        functools.partial(_matmul_residual_kernel, scale=scale),
        grid=(t // tm, n // tn),
        in_specs=[
            pl.BlockSpec((tm, k), lambda i, j: (i, 0)),
            pl.BlockSpec((k, tn), lambda i, j: (0, j)),
            pl.BlockSpec((tm, tn), lambda i, j: (i, j)),
        ],
        out_specs=pl.BlockSpec((tm, tn), lambda i, j: (i, j)),
        out_shape=jax.ShapeDtypeStruct((t, n), F32),
        compiler_params=_params("parallel", "arbitrary"),
    )(a, w, res)


def _qkv_kernel(x_ref, g_ref, w_ref, o_ref, h_ref, *, q_blocks, q_scale):
    j = pl.program_id(1)

    @pl.when(j == 0)
    def _():
        _rmsnorm_rows(h_ref, x_ref, g_ref)

    acc = jnp.dot(h_ref[...], w_ref[...], preferred_element_type=F32)
    scale = jnp.where(j < q_blocks, q_scale, 1.0).astype(F32)
    o_ref[...] = (acc * scale).astype(o_ref.dtype)


def _qkv_forget_kernel(x_ref, g_ref, w_ref, wf_ref, o_ref, f_ref, h_ref, *, q_blocks, q_scale):
    j = pl.program_id(1)

    @pl.when(j == 0)
    def _():
        _rmsnorm_rows(h_ref, x_ref, g_ref)
        f_ref[...] = jnp.dot(h_ref[...], wf_ref[...], preferred_element_type=F32)

    acc = jnp.dot(h_ref[...], w_ref[...], preferred_element_type=F32)
    scale = jnp.where(j < q_blocks, q_scale, 1.0).astype(F32)
    o_ref[...] = (acc * scale).astype(o_ref.dtype)


def _qkv_proj(x, g, w, wf=None, *, out_dtype, tm, tn):
    t, d = x.shape
    n = w.shape[1]
    kw = dict(q_blocks=(n // 3) // tn, q_scale=HEAD_DIM ** -0.5)
    in_specs = [
        pl.BlockSpec((tm, d), lambda i, j: (i, 0)),
        pl.BlockSpec((1, d), lambda i, j: (0, 0)),
        pl.BlockSpec((d, tn), lambda i, j: (0, j)),
    ]
    qkv_spec = pl.BlockSpec((tm, tn), lambda i, j: (i, j))
    qkv_shape = jax.ShapeDtypeStruct((t, n), out_dtype)
    scratch = [pltpu.VMEM((tm, d), BF16)]
    if wf is None:
        return pl.pallas_call(
            functools.partial(_qkv_kernel, **kw),
            grid=(t // tm, n // tn), in_specs=in_specs, out_specs=qkv_spec,
            out_shape=qkv_shape, scratch_shapes=scratch,
            compiler_params=_params("parallel", "arbitrary"),
        )(x, g.reshape(1, d), w)
    nf = wf.shape[1]
    return pl.pallas_call(
        functools.partial(_qkv_forget_kernel, **kw),
        grid=(t // tm, n // tn),
        in_specs=in_specs + [pl.BlockSpec((d, nf), lambda i, j: (0, 0))],
        out_specs=[qkv_spec, pl.BlockSpec((tm, nf), lambda i, j: (i, 0))],
        out_shape=[qkv_shape, jax.ShapeDtypeStruct((t, nf), F32)],
        scratch_shapes=scratch,
        compiler_params=_params("parallel", "arbitrary"),
    )(x, g.reshape(1, d), w, wf)


CUMSUM_ROWS = 256


def _split3(x):
    hi = x.astype(BF16)
    r1 = x - hi.astype(F32)
    mid = r1.astype(BF16)
    lo = (r1 - mid.astype(F32)).astype(BF16)
    return hi, mid, lo


def _forget_cumsum_kernel(f_ref, b_ref, c_ref):
    s = f_ref.shape[0]
    row = lax.broadcasted_iota(jnp.int32, (CUMSUM_ROWS, CUMSUM_ROWS), 0)
    col = lax.broadcasted_iota(jnp.int32, (CUMSUM_ROWS, CUMSUM_ROWS), 1)
    tri = (col <= row).astype(BF16)
    carry = jnp.zeros((1, f_ref.shape[1]), F32)
    for c in range(s // CUMSUM_ROWS):
        z = f_ref[pl.ds(c * CUMSUM_ROWS, CUMSUM_ROWS), :] + b_ref[...]
        log_f = jnp.minimum(z, 0.0) - jnp.log(1.0 + jnp.exp(-jnp.abs(z)))
        hi, mid, lo = _split3(log_f)
        part = (jnp.dot(tri, hi, preferred_element_type=F32)
                + jnp.dot(tri, mid, preferred_element_type=F32)
                + jnp.dot(tri, lo, preferred_element_type=F32))
        out = part + carry
        c_ref[pl.ds(c * CUMSUM_ROWS, CUMSUM_ROWS), :] = out
        carry = out[CUMSUM_ROWS - 1:CUMSUM_ROWS, :]


def _forget_cumsum(f, b, *, batch):
    t, nf = f.shape
    s = t // batch
    return pl.pallas_call(
        _forget_cumsum_kernel,
        grid=(batch,),
        in_specs=[pl.BlockSpec((s, nf), lambda b_: (b_, 0)),
                  pl.BlockSpec((1, nf), lambda b_: (0, 0))],
        out_specs=pl.BlockSpec((s, nf), lambda b_: (b_, 0)),
        out_shape=jax.ShapeDtypeStruct((t, nf), F32),
        compiler_params=_params("parallel"),
    )(f, b)


def _fox_kernel(q_ref, k_ref, v_ref, ccol_ref, crow_ref, o_ref, *, tq, tk):
    h = pl.program_id(1)
    i = pl.program_id(2)
    q = q_ref[...]
    lane = lax.broadcasted_iota(jnp.int32, ccol_ref.shape, 1)
    cq = jnp.sum(jnp.where(lane == h, ccol_ref[...], 0.0), axis=1, keepdims=True)
    row = lax.broadcasted_iota(jnp.int32, (tq, tk), 0)
    col = lax.broadcasted_iota(jnp.int32, (tq, tk), 1)
    n_diag = tq // tk

    def step(j, carry, diag):
        m, l, acc = carry
        k0 = pl.multiple_of(j * tk, tk)
        k = k_ref[pl.ds(k0, tk), :]
        v = v_ref[pl.ds(k0, tk), :]
        s = lax.dot_general(q, k, (((1,), (1,)), ((), ())), preferred_element_type=F32)
        s = s + cq - crow_ref[:, pl.ds(k0, tk)]
        if diag is not None:
            s = jnp.where(col + diag * tk <= row, s, NEG_BIG)
        m_new = jnp.maximum(m, jnp.max(s, axis=1, keepdims=True))
        p = jnp.exp(s - m_new)
        alpha = jnp.exp(m - m_new)
        l = alpha * l + jnp.sum(p, axis=1, keepdims=True)
        acc = alpha * acc + jnp.dot(p.astype(BF16), v, preferred_element_type=F32)
        return m_new, l, acc

    carry = (jnp.full((tq, 1), NEG_BIG, F32), jnp.zeros((tq, 1), F32),
             jnp.zeros((tq, HEAD_DIM), F32))
    def full_blocks(n, c):
        for u in range(n_diag):
            c = step(n * n_diag + u, c, None)
        return c

    carry = lax.fori_loop(0, i, full_blocks, carry)
    for d in range(n_diag):
        carry = step(i * n_diag + d, carry, d)
    _, l, acc = carry
    o_ref[...] = (acc / l).astype(o_ref.dtype)


def _fox_attention(qkv, ccol, crow, *, batch, tq, tk):
    t, n3 = qkv.shape
    s = t // batch
    nh = n3 // (3 * HEAD_DIM)
    qkv3 = qkv.reshape(batch, s, n3)
    ccol3 = ccol.reshape(batch, s, ccol.shape[1])
    out = pl.pallas_call(
        functools.partial(_fox_kernel, tq=tq, tk=tk),
        grid=(batch, nh, s // tq),
        in_specs=[
            pl.BlockSpec((None, tq, HEAD_DIM), lambda b, h, i: (b, i, h)),
            pl.BlockSpec((None, s, HEAD_DIM), lambda b, h, i: (b, 0, nh + h)),
            pl.BlockSpec((None, s, HEAD_DIM), lambda b, h, i: (b, 0, 2 * nh + h)),
            pl.BlockSpec((None, tq, ccol.shape[1]), lambda b, h, i: (b, i, 0)),
            pl.BlockSpec((None, 1, s), lambda b, h, i: (b * nh + h, 0, 0)),
        ],
        out_specs=pl.BlockSpec((None, tq, HEAD_DIM), lambda b, h, i: (b, i, h)),
        out_shape=jax.ShapeDtypeStruct((batch, s, nh * HEAD_DIM), BF16),
        compiler_params=_params("parallel", "parallel", "arbitrary"),
    )(qkv3, qkv3, qkv3, ccol3, crow)
    return out.reshape(t, nh * HEAD_DIM)


SUFFIX_BLOCK = 128


def _stick_kernel(q_ref, k_ref, v_ref, o_ref, *, tq, tk):
    i = pl.program_id(2)
    sb = SUFFIX_BLOCK
    q = q_ref[...]
    row = lax.broadcasted_iota(jnp.int32, (tq, tk), 0)
    col = lax.broadcasted_iota(jnp.int32, (tq, tk), 1)
    jj = lax.broadcasted_iota(jnp.int32, (2 * sb, 2 * sb), 0) % sb
    ss = lax.broadcasted_iota(jnp.int32, (2 * sb, 2 * sb), 1)
    suffix = ((ss >= sb) | (jj > ss)).astype(BF16)
    n_diag = tq // tk

    def step(j, carry, diag):
        tail, acc = carry
        k0 = pl.multiple_of(j * tk, tk)
        k = k_ref[pl.ds(k0, tk), :]
        v = v_ref[pl.ds(k0, tk), :]
        z = lax.dot_general(q, k, (((1,), (1,)), ((), ())), preferred_element_type=F32)
        log_beta = jnp.minimum(z, 0.0) - jnp.log(1.0 + jnp.exp(-jnp.abs(z)))
        log_rest = log_beta - z
        if diag is not None:
            causal = col + diag * tk < row
            log_rest = jnp.where(causal, log_rest, 0.0)
        hi = log_rest.astype(BF16)
        lo = (log_rest - hi.astype(F32)).astype(BF16)
        after = [None] * (tk // sb)
        for u in reversed(range(tk // sb)):
            cols = slice(u * sb, (u + 1) * sb)
            sums = jnp.dot(jnp.concatenate([hi[:, cols], lo[:, cols]], axis=1), suffix,
                           preferred_element_type=F32)
            after[u] = tail + sums[:, :sb]
            tail = tail + sums[:, sb:]
        a = jnp.exp(log_beta + jnp.concatenate(after, axis=1))
        if diag is not None:
            a = jnp.where(causal, a, 0.0)
        acc = acc + jnp.dot(a.astype(BF16), v, preferred_element_type=F32)
        return tail, acc

    carry = (jnp.zeros((tq, sb), F32), jnp.zeros((tq, HEAD_DIM), F32))
    for d in reversed(range(n_diag)):
        carry = step(i * n_diag + d, carry, d)
    def full_slabs(n, c):
        for u in range(n_diag):
            c = step((i - n) * n_diag - 1 - u, c, None)
        return c

    _, acc = lax.fori_loop(0, i, full_slabs, carry)
    o_ref[...] = acc.astype(o_ref.dtype)


def _stick_attention(qkv, *, batch, tq, tk):
    t, n3 = qkv.shape
    s = t // batch
    nh = n3 // (3 * HEAD_DIM)
    qkv3 = qkv.reshape(batch, s, n3)
    out = pl.pallas_call(
        functools.partial(_stick_kernel, tq=tq, tk=tk),
        grid=(batch, nh, s // tq),
        in_specs=[
            pl.BlockSpec((None, tq, HEAD_DIM), lambda b, h, i: (b, i, h)),
            pl.BlockSpec((None, s, HEAD_DIM), lambda b, h, i: (b, 0, nh + h)),
            pl.BlockSpec((None, s, HEAD_DIM), lambda b, h, i: (b, 0, 2 * nh + h)),
        ],
        out_specs=pl.BlockSpec((None, tq, HEAD_DIM), lambda b, h, i: (b, i, h)),
        out_shape=jax.ShapeDtypeStruct((batch, s, nh * HEAD_DIM), BF16),
        compiler_params=_params("parallel", "parallel", "arbitrary"),
    )(qkv3, qkv3, qkv3)
    return out.reshape(t, nh * HEAD_DIM)


def _t5_bucket_np(dist):
    max_exact = REL_BUCKETS // 2
    d = np.maximum(dist, 1).astype(np.float32)
    ratio = np.log(d / np.float32(max_exact)) / np.float32(math.log(REL_MAX_DIST / max_exact))
    large = max_exact + (ratio * np.float32(REL_BUCKETS - max_exact)).astype(np.int32)
    large = np.minimum(large, REL_BUCKETS - 1)
    return np.where(dist < max_exact, dist, large).astype(np.int32)


def _dilated_bucket_tables():
    qb = ATT_BLOCK
    qi = np.arange(qb)[:, None]
    kj = np.arange(2 * qb)[None, :]
    off = qb + qi - kj
    tables = []
    for window, dilation in DILATED_BRANCHES:
        span = window // dilation
        band = (off >= 0) & (off <= span)
        bucket = _t5_bucket_np(np.clip(off, 0, None) * dilation)
        tables.append(np.where(band, bucket, -1))
    return np.stack(tables).astype(np.int32)


def _dilated_kernel(rel_ref, idx_ref, q_ref, k_ref, v_ref, o_ref, bias_ref, ob_ref, lse_ref):
    h = pl.program_id(0)
    qb = ATT_BLOCK
    seq = q_ref.shape[0]

    @pl.when(pl.program_id(1) == 0)
    def _():
        for g in range(len(DILATED_BRANCHES)):
            idx = idx_ref[g]
            bias = jnp.full(idx.shape, NEG_BIG, F32)
            for b in range(REL_BUCKETS):
                bias = jnp.where(idx == b, rel_ref[h, b], bias)
            bias_ref[g] = bias

    nblk = BLOCKS_PER_STEP
    for g, (_, dil) in enumerate(DILATED_BRANCHES):
        nb = (seq // dil) // qb
        has_prev = nb > 1

        def step(c, carry, g=g, dil=dil, nb=nb, has_prev=has_prev):
            if has_prev:
                r = c // (nb // nblk)
                first = (c % (nb // nblk)) * nblk
                start = first * (qb * dil) + r
                if dil == 1:
                    row_sets = [pl.ds(pl.multiple_of(start, nblk * qb), nblk * qb)]
                else:
                    row_sets = [pl.ds(start, nblk * qb, stride=dil)]
            else:
                row_sets = [pl.ds(c * nblk + u, qb, stride=dil) for u in range(nblk)]

            def load(ref):
                parts = [ref[rs, :] for rs in row_sets]
                return (parts[0] if len(parts) == 1 else jnp.concatenate(parts, axis=0)).astype(BF16)

            q, k, v = load(q_ref), load(k_ref), load(v_ref)
            if has_prev:
                before = jnp.maximum(start - qb * dil, r)
                if dil == 1:
                    rs0 = pl.ds(pl.multiple_of(before, qb), qb)
                else:
                    rs0 = pl.ds(before, qb, stride=dil)
                k_prev = jnp.concatenate([k_ref[rs0, :].astype(BF16), k[:-qb]], axis=0)
                v_prev = jnp.concatenate([v_ref[rs0, :].astype(BF16), v[:-qb]], axis=0)
                bias = bias_ref[g]
                col = lax.broadcasted_iota(jnp.int32, (qb, 2 * qb), 1)
            else:
                bias = bias_ref[g, :, qb:]
            scores = []
            for b in range(nblk):
                rb = slice(b * qb, (b + 1) * qb)
                kb = jnp.concatenate([k_prev[rb], k[rb]], axis=0) if has_prev else k[rb]
                sb = lax.dot_general(q[rb], kb, (((1,), (1,)), ((), ())),
                                     preferred_element_type=F32) + bias
                if has_prev and b == 0:
                    sb = jnp.where(jnp.logical_and(first == 0, col < qb), NEG_BIG, sb)
                scores.append(sb)
            s = jnp.concatenate(scores, axis=0)
            m = jnp.max(s, axis=1, keepdims=True)
            e = jnp.exp(s - m)
            den = jnp.sum(e, axis=1, keepdims=True)
            e = e.astype(BF16)
            outs = []
            for b in range(nblk):
                rb = slice(b * qb, (b + 1) * qb)
                vb = jnp.concatenate([v_prev[rb], v[rb]], axis=0) if has_prev else v[rb]
                outs.append(jnp.dot(e[rb], vb, preferred_element_type=F32))
            out = jnp.concatenate(outs, axis=0) / den
            lse = jnp.broadcast_to(m + jnp.log(den), (nblk * qb, HEAD_DIM))
            rows_per_set = (nblk * qb) // len(row_sets)
            for u, rs in enumerate(row_sets):
                ru = slice(u * rows_per_set, (u + 1) * rows_per_set)
                ob_ref[g, rs, :] = out[ru]
                lse_ref[g, rs, :] = lse[ru]
            return carry

        lax.fori_loop(0, (dil * nb) // nblk, step, 0)

    def merge(c, carry):
        rs = pl.ds(pl.multiple_of(c * qb, qb), qb)
        lse = [lse_ref[g, rs, :] for g in range(len(DILATED_BRANCHES))]
        top = functools.reduce(jnp.maximum, lse)
        w = [jnp.exp(x - top) for x in lse]
        num = sum(w[g] * ob_ref[g, rs, :] for g in range(len(DILATED_BRANCHES)))
        o_ref[rs, :] = (num / sum(w)).astype(o_ref.dtype)
        return carry

    lax.fori_loop(0, seq // qb, merge, 0)


def _dilated_attention(qkv, rel_bias, *, batch):
    t, n3 = qkv.shape
    s = t // batch
    nh = n3 // (3 * HEAD_DIM)
    ng = len(DILATED_BRANCHES)
    qkv3 = qkv.reshape(batch, s, n3)
    idx = jnp.asarray(_dilated_bucket_tables())
    out = pl.pallas_call(
        _dilated_kernel,
        grid=(nh, batch),
        in_specs=[
            pl.BlockSpec(memory_space=pltpu.SMEM),
            pl.BlockSpec(idx.shape, lambda h, b: (0, 0, 0)),
            pl.BlockSpec((None, s, HEAD_DIM), lambda h, b: (b, 0, h)),
            pl.BlockSpec((None, s, HEAD_DIM), lambda h, b: (b, 0, nh + h)),
            pl.BlockSpec((None, s, HEAD_DIM), lambda h, b: (b, 0, 2 * nh + h)),
        ],
        out_specs=pl.BlockSpec((None, s, HEAD_DIM), lambda h, b: (b, 0, h)),
        out_shape=jax.ShapeDtypeStruct((batch, s, nh * HEAD_DIM), BF16),
        scratch_shapes=[pltpu.VMEM((ng, ATT_BLOCK, 2 * ATT_BLOCK), F32),
                        pltpu.VMEM((ng, s, HEAD_DIM), F32),
                        pltpu.VMEM((ng, s, HEAD_DIM), F32)],
        compiler_params=_params("arbitrary", "arbitrary"),
    )(rel_bias, idx, qkv3, qkv3, qkv3)
    return out.reshape(t, nh * HEAD_DIM)


def _final_norm_kernel(x_ref, g_ref, o_ref):
    _rmsnorm_rows(o_ref, x_ref, g_ref)


def _final_norm(x, g, *, tm):
    t, d = x.shape
    return pl.pallas_call(
        _final_norm_kernel,
        grid=(t // tm,),
        in_specs=[pl.BlockSpec((tm, d), lambda i: (i, 0)),
                  pl.BlockSpec((1, d), lambda i: (0, 0))],
        out_specs=pl.BlockSpec((tm, d), lambda i: (i, 0)),
        out_shape=jax.ShapeDtypeStruct((t, d), F32),
        compiler_params=_params("parallel"),
    )(x, g.reshape(1, d))


def _tiles(t):
    tm = min(1024, t)
    return dict(tm=tm, tn=512)


def _ffn(x, g, wg, wu, wd):
    tiles = _tiles(x.shape[0])
    a = _ffn_up(x, g, wg, wu, **tiles)
    return _matmul_residual(a, wd, x, scale=FFN_HALF, **tiles)


def _split_forget(w_in, b_f):
    d3 = w_in.shape[1] - N_HEADS
    wf = jnp.pad(w_in[:, d3:], ((0, 0), (0, HEAD_DIM - N_HEADS))).astype(BF16)
    bf = jnp.pad(b_f, (0, HEAD_DIM - N_HEADS)).reshape(1, HEAD_DIM).astype(F32)
    return w_in[:, :d3].astype(BF16), wf, bf


def kernel(x, norm_g, ffn_w_gate, ffn_w_up, ffn_w_down, w_in_0, b_f_0, w_in_1, w_in_2,
           w_in_3, b_f_3, w_out, rel_bias, final_g):
    batch, seq, d = x.shape
    t = batch * seq
    depth = norm_g.shape[0]
    tiles = _tiles(t)
    wg = ffn_w_gate.astype(BF16)
    wu = ffn_w_up.astype(BF16)
    wd = ffn_w_down.astype(BF16)
    wo = w_out.astype(BF16)
    w_ins = (w_in_0, w_in_1, w_in_2, w_in_3)
    forget_biases = {0: b_f_0, 3: b_f_3}

    xt = x.reshape(t, d)
    for i in range(depth):
        xt = _ffn(xt, norm_g[i, 0], wg[i, 0], wu[i, 0], wd[i, 0])
        mixer = i % 3
        if mixer == 0:
            w_qkv, wf, bf = _split_forget(w_ins[i], forget_biases[i])
            qkv, f = _qkv_proj(xt, norm_g[i, 1], w_qkv, wf, out_dtype=BF16, **tiles)
            ccol = _forget_cumsum(f, bf, batch=batch)
            crow = ccol.reshape(batch, seq, HEAD_DIM)[:, :, :N_HEADS]
            crow = jnp.transpose(crow, (0, 2, 1)).reshape(batch * N_HEADS, 1, seq)
            o = _fox_attention(qkv, ccol, crow, batch=batch, tq=512, tk=256)
        elif mixer == 1:
            qkv = _qkv_proj(xt, norm_g[i, 1], w_ins[i].astype(BF16), out_dtype=F32, **tiles)
            o = _dilated_attention(qkv, rel_bias.astype(F32), batch=batch)
        else:
            qkv = _qkv_proj(xt, norm_g[i, 1], w_ins[i].astype(BF16), out_dtype=BF16, **tiles)
            o = _stick_attention(qkv, batch=batch, tq=512, tk=256)
        xt = _matmul_residual(o, wo[i], xt, scale=1.0, **tiles)
        xt = _ffn(xt, norm_g[i, 2], wg[i, 1], wu[i, 1], wd[i, 1])
    return _final_norm(xt, final_g, tm=tiles["tm"]).reshape(batch, seq, d)
```

```python
import functools
import math

import numpy as np
import jax
import jax.numpy as jnp
from jax import lax
from jax.experimental import pallas as pl
from jax.experimental.pallas import tpu as pltpu

F32 = jnp.float32
BF16 = jnp.bfloat16

N_HEADS = 16
HEAD_DIM = 128
RMS_EPS = 1e-6
FFN_HALF = 0.5
REL_BUCKETS = 32
REL_MAX_DIST = 2048
ATT_BLOCK = 128
BLOCKS_PER_STEP = 4
DILATED_BRANCHES = ((128, 1), (512, 4), (2048, 16))
NEG_BIG = -1e30

VMEM_LIMIT_BYTES = 56 * 1024 * 1024
NORM_ROWS = 32
NORM_UNROLL = 4


def _params(*semantics):
    return pltpu.CompilerParams(dimension_semantics=semantics,
                                vmem_limit_bytes=VMEM_LIMIT_BYTES)


def _rmsnorm_rows(h_ref, x_ref, g_ref):
    rows = x_ref.shape[0]
    g = g_ref[...]

    def body(c, carry):
        r0 = pl.multiple_of(c * NORM_ROWS, NORM_ROWS)
        xc = x_ref[pl.ds(r0, NORM_ROWS), :]
        ms = jnp.mean(xc * xc, axis=-1, keepdims=True)
        h_ref[pl.ds(r0, NORM_ROWS), :] = ((xc * lax.rsqrt(ms + RMS_EPS)) * g).astype(h_ref.dtype)
        return carry

    lax.fori_loop(0, rows // NORM_ROWS, body, 0, unroll=NORM_UNROLL)


def _ffn_up_kernel(x_ref, g_ref, wg_ref, wu_ref, o_ref, h_ref):
    @pl.when(pl.program_id(1) == 0)
    def _():
        _rmsnorm_rows(h_ref, x_ref, g_ref)

    h = h_ref[...]
    gate = jnp.dot(h, wg_ref[...], preferred_element_type=F32)
    up = jnp.dot(h, wu_ref[...], preferred_element_type=F32)
    o_ref[...] = (gate * jax.nn.sigmoid(gate) * up).astype(o_ref.dtype)


def _ffn_up(x, g, wg, wu, *, tm, tn):
    t, d = x.shape
    f = wg.shape[1]
    return pl.pallas_call(
        _ffn_up_kernel,
        grid=(t // tm, f // tn),
        in_specs=[
            pl.BlockSpec((tm, d), lambda i, j: (i, 0)),
            pl.BlockSpec((1, d), lambda i, j: (0, 0)),
            pl.BlockSpec((d, tn), lambda i, j: (0, j)),
            pl.BlockSpec((d, tn), lambda i, j: (0, j)),
        ],
        out_specs=pl.BlockSpec((tm, tn), lambda i, j: (i, j)),
        out_shape=jax.ShapeDtypeStruct((t, f), BF16),
        scratch_shapes=[pltpu.VMEM((tm, d), BF16)],
        compiler_params=_params("parallel", "arbitrary"),
    )(x, g.reshape(1, d), wg, wu)


def _matmul_residual_kernel(a_ref, w_ref, r_ref, o_ref, *, scale):
    acc = jnp.dot(a_ref[...], w_ref[...], preferred_element_type=F32)
    o_ref[...] = r_ref[...] + scale * acc


def _matmul_residual(a, w, res, *, scale, tm, tn):
    t, k = a.shape
    n = w.shape[1]
    return pl.pallas_call(
        functools.partial(_matmul_residual_kernel, scale=scale),
        grid=(t // tm, n // tn),
        in_specs=[
            pl.BlockSpec((tm, k), lambda i, j: (i, 0)),
            pl.BlockSpec((k, tn), lambda i, j: (0, j)),
            pl.BlockSpec((tm, tn), lambda i, j: (i, j)),
        ],
        out_specs=pl.BlockSpec((tm, tn), lambda i, j: (i, j)),
        out_shape=jax.ShapeDtypeStruct((t, n), F32),
        compiler_params=_params("parallel", "arbitrary"),
    )(a, w, res)


def _qkv_kernel(x_ref, g_ref, w_ref, o_ref, h_ref, *, q_blocks, q_scale):
    j = pl.program_id(1)

    @pl.when(j == 0)
    def _():
        _rmsnorm_rows(h_ref, x_ref, g_ref)

    acc = jnp.dot(h_ref[...], w_ref[...], preferred_element_type=F32)
    scale = jnp.where(j < q_blocks, q_scale, 1.0).astype(F32)
    o_ref[...] = (acc * scale).astype(o_ref.dtype)


def _qkv_forget_kernel(x_ref, g_ref, w_ref, wf_ref, o_ref, f_ref, h_ref, *, q_blocks, q_scale):
    j = pl.program_id(1)

    @pl.when(j == 0)
    def _():
        _rmsnorm_rows(h_ref, x_ref, g_ref)
        f_ref[...] = jnp.dot(h_ref[...], wf_ref[...], preferred_element_type=F32)

    acc = jnp.dot(h_ref[...], w_ref[...], preferred_element_type=F32)
    scale = jnp.where(j < q_blocks, q_scale, 1.0).astype(F32)
    o_ref[...] = (acc * scale).astype(o_ref.dtype)


def _qkv_proj(x, g, w, wf=None, *, out_dtype, tm, tn):
    t, d = x.shape
    n = w.shape[1]
    kw = dict(q_blocks=(n // 3) // tn, q_scale=HEAD_DIM ** -0.5)
    in_specs = [
        pl.BlockSpec((tm, d), lambda i, j: (i, 0)),
        pl.BlockSpec((1, d), lambda i, j: (0, 0)),
        pl.BlockSpec((d, tn), lambda i, j: (0, j)),
    ]
    qkv_spec = pl.BlockSpec((tm, tn), lambda i, j: (i, j))
    qkv_shape = jax.ShapeDtypeStruct((t, n), out_dtype)
    scratch = [pltpu.VMEM((tm, d), BF16)]
    if wf is None:
        return pl.pallas_call(
            functools.partial(_qkv_kernel, **kw),
            grid=(t // tm, n // tn), in_specs=in_specs, out_specs=qkv_spec,
            out_shape=qkv_shape, scratch_shapes=scratch,
            compiler_params=_params("parallel", "arbitrary"),
        )(x, g.reshape(1, d), w)
    nf = wf.shape[1]
    return pl.pallas_call(
        functools.partial(_qkv_forget_kernel, **kw),
        grid=(t // tm, n // tn),
        in_specs=in_specs + [pl.BlockSpec((d, nf), lambda i, j: (0, 0))],
        out_specs=[qkv_spec, pl.BlockSpec((tm, nf), lambda i, j: (i, 0))],
        out_shape=[qkv_shape, jax.ShapeDtypeStruct((t, nf), F32)],
        scratch_shapes=scratch,
        compiler_params=_params("parallel", "arbitrary"),
    )(x, g.reshape(1, d), w, wf)


CUMSUM_ROWS = 256


def _split3(x):
    hi = x.astype(BF16)
    r1 = x - hi.astype(F32)
    mid = r1.astype(BF16)
    lo = (r1 - mid.astype(F32)).astype(BF16)
    return hi, mid, lo


def _forget_cumsum_kernel(f_ref, b_ref, c_ref):
    s = f_ref.shape[0]
    row = lax.broadcasted_iota(jnp.int32, (CUMSUM_ROWS, CUMSUM_ROWS), 0)
    col = lax.broadcasted_iota(jnp.int32, (CUMSUM_ROWS, CUMSUM_ROWS), 1)
    tri = (col <= row).astype(BF16)
    carry = jnp.zeros((1, f_ref.shape[1]), F32)
    for c in range(s // CUMSUM_ROWS):
        z = f_ref[pl.ds(c * CUMSUM_ROWS, CUMSUM_ROWS), :] + b_ref[...]
        log_f = jnp.minimum(z, 0.0) - jnp.log(1.0 + jnp.exp(-jnp.abs(z)))
        hi, mid, lo = _split3(log_f)
        part = (jnp.dot(tri, hi, preferred_element_type=F32)
                + jnp.dot(tri, mid, preferred_element_type=F32)
                + jnp.dot(tri, lo, preferred_element_type=F32))
        out = part + carry
        c_ref[pl.ds(c * CUMSUM_ROWS, CUMSUM_ROWS), :] = out
        carry = out[CUMSUM_ROWS - 1:CUMSUM_ROWS, :]


def _forget_cumsum(f, b, *, batch):
    t, nf = f.shape
    s = t // batch
    return pl.pallas_call(
        _forget_cumsum_kernel,
        grid=(batch,),
        in_specs=[pl.BlockSpec((s, nf), lambda b_: (b_, 0)),
                  pl.BlockSpec((1, nf), lambda b_: (0, 0))],
        out_specs=pl.BlockSpec((s, nf), lambda b_: (b_, 0)),
        out_shape=jax.ShapeDtypeStruct((t, nf), F32),
        compiler_params=_params("parallel"),
    )(f, b)


def _fox_kernel(q_ref, k_ref, v_ref, ccol_ref, crow_ref, o_ref, *, tq, tk):
    h = pl.program_id(1)
    i = pl.program_id(2)
    q = q_ref[...]
    lane = lax.broadcasted_iota(jnp.int32, ccol_ref.shape, 1)
    cq = jnp.sum(jnp.where(lane == h, ccol_ref[...], 0.0), axis=1, keepdims=True)
    row = lax.broadcasted_iota(jnp.int32, (tq, tk), 0)
    col = lax.broadcasted_iota(jnp.int32, (tq, tk), 1)
    n_diag = tq // tk

    def step(j, carry, diag):
        m, l, acc = carry
        k0 = pl.multiple_of(j * tk, tk)
        k = k_ref[pl.ds(k0, tk), :]
        v = v_ref[pl.ds(k0, tk), :]
        s = lax.dot_general(q, k, (((1,), (1,)), ((), ())), preferred_element_type=F32)
        s = s + cq - crow_ref[:, pl.ds(k0, tk)]
        if diag is not None:
            s = jnp.where(col + diag * tk <= row, s, NEG_BIG)
        m_new = jnp.maximum(m, jnp.max(s, axis=1, keepdims=True))
        p = jnp.exp(s - m_new)
        alpha = jnp.exp(m - m_new)
        l = alpha * l + jnp.sum(p, axis=1, keepdims=True)
        acc = alpha * acc + jnp.dot(p.astype(BF16), v, preferred_element_type=F32)
        return m_new, l, acc

    carry = (jnp.full((tq, 1), NEG_BIG, F32), jnp.zeros((tq, 1), F32),
             jnp.zeros((tq, HEAD_DIM), F32))
    def full_blocks(n, c):
        for u in range(n_diag):
            c = step(n * n_diag + u, c, None)
        return c

    carry = lax.fori_loop(0, i, full_blocks, carry)
    for d in range(n_diag):
        carry = step(i * n_diag + d, carry, d)
    _, l, acc = carry
    o_ref[...] = (acc / l).astype(o_ref.dtype)


def _fox_attention(qkv, ccol, crow, *, batch, tq, tk):
    t, n3 = qkv.shape
    s = t // batch
    nh = n3 // (3 * HEAD_DIM)
    qkv3 = qkv.reshape(batch, s, n3)
    ccol3 = ccol.reshape(batch, s, ccol.shape[1])
    out = pl.pallas_call(
        functools.partial(_fox_kernel, tq=tq, tk=tk),
        grid=(batch, nh, s // tq),
        in_specs=[
            pl.BlockSpec((None, tq, HEAD_DIM), lambda b, h, i: (b, i, h)),
            pl.BlockSpec((None, s, HEAD_DIM), lambda b, h, i: (b, 0, nh + h)),
            pl.BlockSpec((None, s, HEAD_DIM), lambda b, h, i: (b, 0, 2 * nh + h)),
            pl.BlockSpec((None, tq, ccol.shape[1]), lambda b, h, i: (b, i, 0)),
            pl.BlockSpec((None, 1, s), lambda b, h, i: (b * nh + h, 0, 0)),
        ],
        out_specs=pl.BlockSpec((None, tq, HEAD_DIM), lambda b, h, i: (b, i, h)),
        out_shape=jax.ShapeDtypeStruct((batch, s, nh * HEAD_DIM), BF16),
        compiler_params=_params("parallel", "parallel", "arbitrary"),
    )(qkv3, qkv3, qkv3, ccol3, crow)
    return out.reshape(t, nh * HEAD_DIM)


SUFFIX_BLOCK = 128


def _stick_kernel(q_ref, k_ref, v_ref, o_ref, *, tq, tk):
    i = pl.program_id(2)
    sb = SUFFIX_BLOCK
    q = q_ref[...]
    jj = lax.broadcasted_iota(jnp.int32, (2 * sb, 2 * sb), 0) % sb
    ss = lax.broadcasted_iota(jnp.int32, (2 * sb, 2 * sb), 1)
    suffix = ((ss >= sb) | (jj > ss)).astype(BF16)
    n_diag = tq // tk

    def step(j, carry, diag):
        r0 = 0 if diag is None else diag * tk
        tail, acc = (x[r0:] for x in carry)
        k0 = pl.multiple_of(j * tk, tk)
        k = k_ref[pl.ds(k0, tk), :]
        v = v_ref[pl.ds(k0, tk), :]
        z = lax.dot_general(q[r0:], k, (((1,), (1,)), ((), ())), preferred_element_type=F32)
        log_beta = jnp.minimum(z, 0.0) - jnp.log(1.0 + jnp.exp(-jnp.abs(z)))
        log_rest = log_beta - z
        if diag is not None:
            causal = (lax.broadcasted_iota(jnp.int32, (tq - r0, tk), 1)
                      < lax.broadcasted_iota(jnp.int32, (tq - r0, tk), 0))
            log_rest = jnp.where(causal, log_rest, 0.0)
        hi = log_rest.astype(BF16)
        lo = (log_rest - hi.astype(F32)).astype(BF16)
        after = [None] * (tk // sb)
        for u in reversed(range(tk // sb)):
            cols = slice(u * sb, (u + 1) * sb)
            sums = jnp.dot(jnp.concatenate([hi[:, cols], lo[:, cols]], axis=1), suffix,
                           preferred_element_type=F32)
            after[u] = tail + sums[:, :sb]
            tail = tail + sums[:, sb:]
        a = jnp.exp(log_beta + jnp.concatenate(after, axis=1))
        if diag is not None:
            a = jnp.where(causal, a, 0.0)
        acc = acc + jnp.dot(a.astype(BF16), v, preferred_element_type=F32)
        new = (tail, acc)
        if r0:
            new = tuple(jnp.concatenate([old[:r0], part], axis=0) for old, part in zip(carry, new))
        return new

    carry = (jnp.zeros((tq, sb), F32), jnp.zeros((tq, HEAD_DIM), F32))
    for d in reversed(range(n_diag)):
        carry = step(i * n_diag + d, carry, d)
    def full_slabs(n, c):
        for u in range(n_diag):
            c = step((i - n) * n_diag - 1 - u, c, None)
        return c

    _, acc = lax.fori_loop(0, i, full_slabs, carry)
    o_ref[...] = acc.astype(o_ref.dtype)


def _stick_attention(qkv, *, batch, tq, tk):
    t, n3 = qkv.shape
    s = t // batch
    nh = n3 // (3 * HEAD_DIM)
    qkv3 = qkv.reshape(batch, s, n3)
    out = pl.pallas_call(
        functools.partial(_stick_kernel, tq=tq, tk=tk),
        grid=(batch, nh, s // tq),
        in_specs=[
            pl.BlockSpec((None, tq, HEAD_DIM), lambda b, h, i: (b, i, h)),
            pl.BlockSpec((None, s, HEAD_DIM), lambda b, h, i: (b, 0, nh + h)),
            pl.BlockSpec((None, s, HEAD_DIM), lambda b, h, i: (b, 0, 2 * nh + h)),
        ],
        out_specs=pl.BlockSpec((None, tq, HEAD_DIM), lambda b, h, i: (b, i, h)),
        out_shape=jax.ShapeDtypeStruct((batch, s, nh * HEAD_DIM), BF16),
        compiler_params=_params("parallel", "parallel", "arbitrary"),
    )(qkv3, qkv3, qkv3)
    return out.reshape(t, nh * HEAD_DIM)


def _t5_bucket_np(dist):
    max_exact = REL_BUCKETS // 2
    d = np.maximum(dist, 1).astype(np.float32)
    ratio = np.log(d / np.float32(max_exact)) / np.float32(math.log(REL_MAX_DIST / max_exact))
    large = max_exact + (ratio * np.float32(REL_BUCKETS - max_exact)).astype(np.int32)
    large = np.minimum(large, REL_BUCKETS - 1)
    return np.where(dist < max_exact, dist, large).astype(np.int32)


def _dilated_bucket_tables():
    qb = ATT_BLOCK
    qi = np.arange(qb)[:, None]
    kj = np.arange(2 * qb)[None, :]
    off = qb + qi - kj
    tables = []
    for window, dilation in DILATED_BRANCHES:
        span = window // dilation
        band = (off >= 0) & (off <= span)
        bucket = _t5_bucket_np(np.clip(off, 0, None) * dilation)
        tables.append(np.where(band, bucket, -1))
    return np.stack(tables).astype(np.int32)


def _dilated_kernel(rel_ref, idx_ref, q_ref, k_ref, v_ref, o_ref, bias_ref, ob_ref, lse_ref):
    h = pl.program_id(0)
    qb = ATT_BLOCK
    seq = q_ref.shape[0]

    @pl.when(pl.program_id(1) == 0)
    def _():
        for g in range(len(DILATED_BRANCHES)):
            idx = idx_ref[g]
            bias = jnp.full(idx.shape, NEG_BIG, F32)
            for b in range(REL_BUCKETS):
                bias = jnp.where(idx == b, rel_ref[h, b], bias)
            bias_ref[g] = bias

    nblk = BLOCKS_PER_STEP
    for g, (_, dil) in enumerate(DILATED_BRANCHES):
        nb = (seq // dil) // qb
        has_prev = nb > 1

        def step(c, carry, g=g, dil=dil, nb=nb, has_prev=has_prev):
            if has_prev:
                r = c // (nb // nblk)
                first = (c % (nb // nblk)) * nblk
                start = first * (qb * dil) + r
                if dil == 1:
                    row_sets = [pl.ds(pl.multiple_of(start, nblk * qb), nblk * qb)]
                else:
                    row_sets = [pl.ds(start, nblk * qb, stride=dil)]
            else:
                row_sets = [pl.ds(c * nblk + u, qb, stride=dil) for u in range(nblk)]

            def load(ref):
                parts = [ref[rs, :] for rs in row_sets]
                return (parts[0] if len(parts) == 1 else jnp.concatenate(parts, axis=0)).astype(BF16)

            q, k, v = load(q_ref), load(k_ref), load(v_ref)
            if has_prev:
                before = jnp.maximum(start - qb * dil, r)
                if dil == 1:
                    rs0 = pl.ds(pl.multiple_of(before, qb), qb)
                else:
                    rs0 = pl.ds(before, qb, stride=dil)
                k_prev = jnp.concatenate([k_ref[rs0, :].astype(BF16), k[:-qb]], axis=0)
                v_prev = jnp.concatenate([v_ref[rs0, :].astype(BF16), v[:-qb]], axis=0)
                bias = bias_ref[g]
                col = lax.broadcasted_iota(jnp.int32, (qb, 2 * qb), 1)
            else:
                bias = bias_ref[g, :, qb:]
            scores = []
            for b in range(nblk):
                rb = slice(b * qb, (b + 1) * qb)
                kb = jnp.concatenate([k_prev[rb], k[rb]], axis=0) if has_prev else k[rb]
                sb = lax.dot_general(q[rb], kb, (((1,), (1,)), ((), ())),
                                     preferred_element_type=F32) + bias
                if has_prev and b == 0:
                    sb = jnp.where(jnp.logical_and(first == 0, col < qb), NEG_BIG, sb)
                scores.append(sb)
            s = jnp.concatenate(scores, axis=0)
            m = jnp.max(s, axis=1, keepdims=True)
            e = jnp.exp(s - m)
            den = jnp.sum(e, axis=1, keepdims=True)
            e = e.astype(BF16)
            outs = []
            for b in range(nblk):
                rb = slice(b * qb, (b + 1) * qb)
                vb = jnp.concatenate([v_prev[rb], v[rb]], axis=0) if has_prev else v[rb]
                outs.append(jnp.dot(e[rb], vb, preferred_element_type=F32))
            out = jnp.concatenate(outs, axis=0) / den
            lse = jnp.broadcast_to(m + jnp.log(den), (nblk * qb, HEAD_DIM))
            rows_per_set = (nblk * qb) // len(row_sets)
            for u, rs in enumerate(row_sets):
                ru = slice(u * rows_per_set, (u + 1) * rows_per_set)
                ob_ref[g, rs, :] = out[ru]
                lse_ref[g, rs, :] = lse[ru]
            return carry

        lax.fori_loop(0, (dil * nb) // nblk, step, 0)

    def merge(c, carry):
        rs = pl.ds(pl.multiple_of(c * qb, qb), qb)
        lse = [lse_ref[g, rs, :] for g in range(len(DILATED_BRANCHES))]
        top = functools.reduce(jnp.maximum, lse)
        w = [jnp.exp(x - top) for x in lse]
        num = sum(w[g] * ob_ref[g, rs, :] for g in range(len(DILATED_BRANCHES)))
        o_ref[rs, :] = (num / sum(w)).astype(o_ref.dtype)
        return carry

    lax.fori_loop(0, seq // qb, merge, 0)


def _dilated_attention(qkv, rel_bias, *, batch):
    t, n3 = qkv.shape
    s = t // batch
    nh = n3 // (3 * HEAD_DIM)
    ng = len(DILATED_BRANCHES)
    qkv3 = qkv.reshape(batch, s, n3)
    idx = jnp.asarray(_dilated_bucket_tables())
    out = pl.pallas_call(
        _dilated_kernel,
        grid=(nh, batch),
        in_specs=[
            pl.BlockSpec(memory_space=pltpu.SMEM),
            pl.BlockSpec(idx.shape, lambda h, b: (0, 0, 0)),
            pl.BlockSpec((None, s, HEAD_DIM), lambda h, b: (b, 0, h)),
            pl.BlockSpec((None, s, HEAD_DIM), lambda h, b: (b, 0, nh + h)),
            pl.BlockSpec((None, s, HEAD_DIM), lambda h, b: (b, 0, 2 * nh + h)),
        ],
        out_specs=pl.BlockSpec((None, s, HEAD_DIM), lambda h, b: (b, 0, h)),
        out_shape=jax.ShapeDtypeStruct((batch, s, nh * HEAD_DIM), BF16),
        scratch_shapes=[pltpu.VMEM((ng, ATT_BLOCK, 2 * ATT_BLOCK), F32),
                        pltpu.VMEM((ng, s, HEAD_DIM), F32),
                        pltpu.VMEM((ng, s, HEAD_DIM), F32)],
        compiler_params=_params("arbitrary", "arbitrary"),
    )(rel_bias, idx, qkv3, qkv3, qkv3)
    return out.reshape(t, nh * HEAD_DIM)


def _final_norm_kernel(x_ref, g_ref, o_ref):
    _rmsnorm_rows(o_ref, x_ref, g_ref)


def _final_norm(x, g, *, tm):
    t, d = x.shape
    return pl.pallas_call(
        _final_norm_kernel,
        grid=(t // tm,),
        in_specs=[pl.BlockSpec((tm, d), lambda i: (i, 0)),
                  pl.BlockSpec((1, d), lambda i: (0, 0))],
        out_specs=pl.BlockSpec((tm, d), lambda i: (i, 0)),
        out_shape=jax.ShapeDtypeStruct((t, d), F32),
        compiler_params=_params("parallel"),
    )(x, g.reshape(1, d))


def _tiles(t, k_dim=None):
    tm = min(1024, t)
    return dict(tm=tm, tn=1024 if k_dim is not None and k_dim <= 2048 else 512)


def _ffn(x, g, wg, wu, wd):
    tiles = _tiles(x.shape[0])
    a = _ffn_up(x, g, wg, wu, **tiles)
    return _matmul_residual(a, wd, x, scale=FFN_HALF, **tiles)


def _split_forget(w_in, b_f):
    d3 = w_in.shape[1] - N_HEADS
    wf = jnp.pad(w_in[:, d3:], ((0, 0), (0, HEAD_DIM - N_HEADS))).astype(BF16)
    bf = jnp.pad(b_f, (0, HEAD_DIM - N_HEADS)).reshape(1, HEAD_DIM).astype(F32)
    return w_in[:, :d3].astype(BF16), wf, bf


def kernel(x, norm_g, ffn_w_gate, ffn_w_up, ffn_w_down, w_in_0, b_f_0, w_in_1, w_in_2,
           w_in_3, b_f_3, w_out, rel_bias, final_g):
    batch, seq, d = x.shape
    t = batch * seq
    depth = norm_g.shape[0]
    tiles = _tiles(t, k_dim=d)
    wg = ffn_w_gate.astype(BF16)
    wu = ffn_w_up.astype(BF16)
    wd = ffn_w_down.astype(BF16)
    wo = w_out.astype(BF16)
    w_ins = (w_in_0, w_in_1, w_in_2, w_in_3)
    forget_biases = {0: b_f_0, 3: b_f_3}

    xt = x.reshape(t, d)
    for i in range(depth):
        xt = _ffn(xt, norm_g[i, 0], wg[i, 0], wu[i, 0], wd[i, 0])
        mixer = i % 3
        if mixer == 0:
            w_qkv, wf, bf = _split_forget(w_ins[i], forget_biases[i])
            qkv, f = _qkv_proj(xt, norm_g[i, 1], w_qkv, wf, out_dtype=BF16, **tiles)
            ccol = _forget_cumsum(f, bf, batch=batch)
            crow = ccol.reshape(batch, seq, HEAD_DIM)[:, :, :N_HEADS]
            crow = jnp.transpose(crow, (0, 2, 1)).reshape(batch * N_HEADS, 1, seq)
            o = _fox_attention(qkv, ccol, crow, batch=batch, tq=512, tk=256)
        elif mixer == 1:
            qkv = _qkv_proj(xt, norm_g[i, 1], w_ins[i].astype(BF16), out_dtype=F32, **tiles)
            o = _dilated_attention(qkv, rel_bias.astype(F32), batch=batch)
        else:
            qkv = _qkv_proj(xt, norm_g[i, 1], w_ins[i].astype(BF16), out_dtype=BF16, **tiles)
            o = _stick_attention(qkv, batch=batch, tq=512, tk=256)
        xt = _matmul_residual(o, wo[i], xt, scale=1.0, **tiles)
        xt = _ffn(xt, norm_g[i, 2], wg[i, 1], wu[i, 1], wd[i, 1])
    return _final_norm(xt, final_g, tm=tiles["tm"]).reshape(batch, seq, d)
```

```python
import functools
import math

import numpy as np
import jax
import jax.numpy as jnp
from jax import lax
from jax.experimental import pallas as pl
from jax.experimental.pallas import tpu as pltpu

F32 = jnp.float32
BF16 = jnp.bfloat16

N_HEADS = 16
HEAD_DIM = 128
RMS_EPS = 1e-6
FFN_HALF = 0.5
REL_BUCKETS = 32
REL_MAX_DIST = 2048
ATT_BLOCK = 128
BLOCKS_PER_STEP = 4
DILATED_BRANCHES = ((128, 1), (512, 4), (2048, 16))
NEG_BIG = -1e30

VMEM_LIMIT_BYTES = 56 * 1024 * 1024
NORM_ROWS = 32
NORM_UNROLL = 4


def _params(*semantics):
    return pltpu.CompilerParams(dimension_semantics=semantics,
                                vmem_limit_bytes=VMEM_LIMIT_BYTES)


def _rmsnorm_rows(h_ref, x_ref, g_ref):
    rows = x_ref.shape[0]
    g = g_ref[...]

    def body(c, carry):
        r0 = pl.multiple_of(c * NORM_ROWS, NORM_ROWS)
        xc = x_ref[pl.ds(r0, NORM_ROWS), :]
        ms = jnp.mean(xc * xc, axis=-1, keepdims=True)
        h_ref[pl.ds(r0, NORM_ROWS), :] = ((xc * lax.rsqrt(ms + RMS_EPS)) * g).astype(h_ref.dtype)
        return carry

    lax.fori_loop(0, rows // NORM_ROWS, body, 0, unroll=NORM_UNROLL)


def _ffn_up_kernel(x_ref, g_ref, wg_ref, wu_ref, o_ref, h_ref):
    @pl.when(pl.program_id(1) == 0)
    def _():
        _rmsnorm_rows(h_ref, x_ref, g_ref)

    h = h_ref[...]
    gate = jnp.dot(h, wg_ref[...], preferred_element_type=F32)
    up = jnp.dot(h, wu_ref[...], preferred_element_type=F32)
    o_ref[...] = (gate * jax.nn.sigmoid(gate) * up).astype(o_ref.dtype)


def _ffn_up(x, g, wg, wu, *, tm, tn):
    t, d = x.shape
    f = wg.shape[1]
    return pl.pallas_call(
        _ffn_up_kernel,
        grid=(t // tm, f // tn),
        in_specs=[
            pl.BlockSpec((tm, d), lambda i, j: (i, 0)),
            pl.BlockSpec((1, d), lambda i, j: (0, 0)),
            pl.BlockSpec((d, tn), lambda i, j: (0, j)),
            pl.BlockSpec((d, tn), lambda i, j: (0, j)),
        ],
        out_specs=pl.BlockSpec((tm, tn), lambda i, j: (i, j)),
        out_shape=jax.ShapeDtypeStruct((t, f), BF16),
        scratch_shapes=[pltpu.VMEM((tm, d), BF16)],
        compiler_params=_params("parallel", "arbitrary"),
    )(x, g.reshape(1, d), wg, wu)


def _matmul_residual_kernel(a_ref, w_ref, r_ref, o_ref, *, scale):
    acc = jnp.dot(a_ref[...], w_ref[...], preferred_element_type=F32)
    o_ref[...] = r_ref[...] + scale * acc


def _matmul_residual(a, w, res, *, scale, tm, tn):
    t, k = a.shape
    n = w.shape[1]
    return pl.pallas_call(
        functools.partial(_matmul_residual_kernel, scale=scale),
        grid=(t // tm, n // tn),
        in_specs=[
            pl.BlockSpec((tm, k), lambda i, j: (i, 0)),
            pl.BlockSpec((k, tn), lambda i, j: (0, j)),
            pl.BlockSpec((tm, tn), lambda i, j: (i, j)),
        ],
        out_specs=pl.BlockSpec((tm, tn), lambda i, j: (i, j)),
        out_shape=jax.ShapeDtypeStruct((t, n), F32),
        compiler_params=_params("parallel", "arbitrary"),
    )(a, w, res)


def _qkv_kernel(x_ref, g_ref, w_ref, o_ref, h_ref, *, q_blocks, q_scale):
    j = pl.program_id(1)

    @pl.when(j == 0)
    def _():
        _rmsnorm_rows(h_ref, x_ref, g_ref)

    acc = jnp.dot(h_ref[...], w_ref[...], preferred_element_type=F32)
    scale = jnp.where(j < q_blocks, q_scale, 1.0).astype(F32)
    o_ref[...] = (acc * scale).astype(o_ref.dtype)


def _qkv_forget_kernel(x_ref, g_ref, w_ref, wf_ref, o_ref, f_ref, h_ref, *, q_blocks, q_scale):
    j = pl.program_id(1)

    @pl.when(j == 0)
    def _():
        _rmsnorm_rows(h_ref, x_ref, g_ref)
        f_ref[...] = jnp.dot(h_ref[...], wf_ref[...], preferred_element_type=F32)

    acc = jnp.dot(h_ref[...], w_ref[...], preferred_element_type=F32)
    scale = jnp.where(j < q_blocks, q_scale, 1.0).astype(F32)
    o_ref[...] = (acc * scale).astype(o_ref.dtype)


def _qkv_proj(x, g, w, wf=None, *, out_dtype, tm, tn):
    t, d = x.shape
    n = w.shape[1]
    kw = dict(q_blocks=(n // 3) // tn, q_scale=HEAD_DIM ** -0.5)
    in_specs = [
        pl.BlockSpec((tm, d), lambda i, j: (i, 0)),
        pl.BlockSpec((1, d), lambda i, j: (0, 0)),
        pl.BlockSpec((d, tn), lambda i, j: (0, j)),
    ]
    qkv_spec = pl.BlockSpec((tm, tn), lambda i, j: (i, j))
    qkv_shape = jax.ShapeDtypeStruct((t, n), out_dtype)
    scratch = [pltpu.VMEM((tm, d), BF16)]
    if wf is None:
        return pl.pallas_call(
            functools.partial(_qkv_kernel, **kw),
            grid=(t // tm, n // tn), in_specs=in_specs, out_specs=qkv_spec,
            out_shape=qkv_shape, scratch_shapes=scratch,
            compiler_params=_params("parallel", "arbitrary"),
        )(x, g.reshape(1, d), w)
    nf = wf.shape[1]
    return pl.pallas_call(
        functools.partial(_qkv_forget_kernel, **kw),
        grid=(t // tm, n // tn),
        in_specs=in_specs + [pl.BlockSpec((d, nf), lambda i, j: (0, 0))],
        out_specs=[qkv_spec, pl.BlockSpec((tm, nf), lambda i, j: (i, 0))],
        out_shape=[qkv_shape, jax.ShapeDtypeStruct((t, nf), F32)],
        scratch_shapes=scratch,
        compiler_params=_params("parallel", "arbitrary"),
    )(x, g.reshape(1, d), w, wf)


CUMSUM_ROWS = 256


def _split3(x):
    hi = x.astype(BF16)
    r1 = x - hi.astype(F32)
    mid = r1.astype(BF16)
    lo = (r1 - mid.astype(F32)).astype(BF16)
    return hi, mid, lo


def _forget_cumsum_kernel(f_ref, b_ref, c_ref):
    s = f_ref.shape[0]
    row = lax.broadcasted_iota(jnp.int32, (CUMSUM_ROWS, CUMSUM_ROWS), 0)
    col = lax.broadcasted_iota(jnp.int32, (CUMSUM_ROWS, CUMSUM_ROWS), 1)
    tri = (col <= row).astype(BF16)
    carry = jnp.zeros((1, f_ref.shape[1]), F32)
    for c in range(s // CUMSUM_ROWS):
        z = f_ref[pl.ds(c * CUMSUM_ROWS, CUMSUM_ROWS), :] + b_ref[...]
        log_f = jnp.minimum(z, 0.0) - jnp.log(1.0 + jnp.exp(-jnp.abs(z)))
        hi, mid, lo = _split3(log_f)
        part = (jnp.dot(tri, hi, preferred_element_type=F32)
                + jnp.dot(tri, mid, preferred_element_type=F32)
                + jnp.dot(tri, lo, preferred_element_type=F32))
        out = part + carry
        c_ref[pl.ds(c * CUMSUM_ROWS, CUMSUM_ROWS), :] = out
        carry = out[CUMSUM_ROWS - 1:CUMSUM_ROWS, :]


def _forget_cumsum(f, b, *, batch):
    t, nf = f.shape
    s = t // batch
    return pl.pallas_call(
        _forget_cumsum_kernel,
        grid=(batch,),
        in_specs=[pl.BlockSpec((s, nf), lambda b_: (b_, 0)),
                  pl.BlockSpec((1, nf), lambda b_: (0, 0))],
        out_specs=pl.BlockSpec((s, nf), lambda b_: (b_, 0)),
        out_shape=jax.ShapeDtypeStruct((t, nf), F32),
        compiler_params=_params("parallel"),
    )(f, b)


def _fox_kernel(q_ref, k_ref, v_ref, ccol_ref, crow_ref, o_ref, *, tq, tk):
    h = pl.program_id(1)
    i = pl.program_id(2)
    q = q_ref[...]
    lane = lax.broadcasted_iota(jnp.int32, ccol_ref.shape, 1)
    cq = jnp.sum(jnp.where(lane == h, ccol_ref[...], 0.0), axis=1, keepdims=True)
    row = lax.broadcasted_iota(jnp.int32, (tq, tk), 0)
    col = lax.broadcasted_iota(jnp.int32, (tq, tk), 1)
    n_diag = tq // tk

    def step(j, carry, diag):
        m, l, acc = carry
        k0 = pl.multiple_of(j * tk, tk)
        k = k_ref[pl.ds(k0, tk), :]
        v = v_ref[pl.ds(k0, tk), :]
        s = lax.dot_general(q, k, (((1,), (1,)), ((), ())), preferred_element_type=F32)
        s = s + cq - crow_ref[:, pl.ds(k0, tk)]
        if diag is not None:
            s = jnp.where(col + diag * tk <= row, s, NEG_BIG)
        m_new = jnp.maximum(m, jnp.max(s, axis=1, keepdims=True))
        p = jnp.exp(s - m_new)
        alpha = jnp.exp(m - m_new)
        l = alpha * l + jnp.sum(p, axis=1, keepdims=True)
        acc = alpha * acc + jnp.dot(p.astype(BF16), v, preferred_element_type=F32)
        return m_new, l, acc

    carry = (jnp.full((tq, 1), NEG_BIG, F32), jnp.zeros((tq, 1), F32),
             jnp.zeros((tq, HEAD_DIM), F32))
    def full_blocks(n, c):
        for u in range(n_diag):
            c = step(n * n_diag + u, c, None)
        return c

    carry = lax.fori_loop(0, i, full_blocks, carry)
    for d in range(n_diag):
        carry = step(i * n_diag + d, carry, d)
    _, l, acc = carry
    o_ref[...] = (acc / l).astype(o_ref.dtype)


def _fox_attention(qkv, ccol, crow, *, batch, tq, tk):
    t, n3 = qkv.shape
    s = t // batch
    nh = n3 // (3 * HEAD_DIM)
    qkv3 = qkv.reshape(batch, s, n3)
    ccol3 = ccol.reshape(batch, s, ccol.shape[1])
    out = pl.pallas_call(
        functools.partial(_fox_kernel, tq=tq, tk=tk),
        grid=(batch, nh, s // tq),
        in_specs=[
            pl.BlockSpec((None, tq, HEAD_DIM), lambda b, h, i: (b, i, h)),
            pl.BlockSpec((None, s, HEAD_DIM), lambda b, h, i: (b, 0, nh + h)),
            pl.BlockSpec((None, s, HEAD_DIM), lambda b, h, i: (b, 0, 2 * nh + h)),
            pl.BlockSpec((None, tq, ccol.shape[1]), lambda b, h, i: (b, i, 0)),
            pl.BlockSpec((None, 1, s), lambda b, h, i: (b * nh + h, 0, 0)),
        ],
        out_specs=pl.BlockSpec((None, tq, HEAD_DIM), lambda b, h, i: (b, i, h)),
        out_shape=jax.ShapeDtypeStruct((batch, s, nh * HEAD_DIM), BF16),
        compiler_params=_params("parallel", "parallel", "arbitrary"),
    )(qkv3, qkv3, qkv3, ccol3, crow)
    return out.reshape(t, nh * HEAD_DIM)


SUFFIX_BLOCK = 128


def _stick_kernel(q_ref, k_ref, v_ref, o_ref, *, tq, tk):
    i = pl.program_id(2)
    sb = SUFFIX_BLOCK
    q = q_ref[...]
    jj = lax.broadcasted_iota(jnp.int32, (2 * sb, 2 * sb), 0) % sb
    ss = lax.broadcasted_iota(jnp.int32, (2 * sb, 2 * sb), 1)
    suffix = ((ss >= sb) | (jj > ss)).astype(BF16)
    n_diag = tq // tk

    def step(j, carry, diag):
        r0 = 0 if diag is None else diag * tk
        tail, acc = (x[r0:] for x in carry)
        k0 = pl.multiple_of(j * tk, tk)
        k = k_ref[pl.ds(k0, tk), :]
        v = v_ref[pl.ds(k0, tk), :]
        z = lax.dot_general(q[r0:], k, (((1,), (1,)), ((), ())), preferred_element_type=F32)
        log_beta = jnp.minimum(z, 0.0) - jnp.log(1.0 + jnp.exp(-jnp.abs(z)))
        log_rest = log_beta - z
        if diag is not None:
            causal = (lax.broadcasted_iota(jnp.int32, (tq - r0, tk), 1)
                      < lax.broadcasted_iota(jnp.int32, (tq - r0, tk), 0))
            log_rest = jnp.where(causal, log_rest, 0.0)
        hi = log_rest.astype(BF16)
        lo = (log_rest - hi.astype(F32)).astype(BF16)
        after = [None] * (tk // sb)
        for u in reversed(range(tk // sb)):
            cols = slice(u * sb, (u + 1) * sb)
            sums = jnp.dot(jnp.concatenate([hi[:, cols], lo[:, cols]], axis=1), suffix,
                           preferred_element_type=F32)
            after[u] = tail + sums[:, :sb]
            tail = tail + sums[:, sb:]
        a = jnp.exp(log_beta + jnp.concatenate(after, axis=1))
        if diag is not None:
            a = jnp.where(causal, a, 0.0)
        acc = acc + jnp.dot(a.astype(BF16), v, preferred_element_type=F32)
        new = (tail, acc)
        if r0:
            new = tuple(jnp.concatenate([old[:r0], part], axis=0) for old, part in zip(carry, new))
        return new

    carry = (jnp.zeros((tq, sb), F32), jnp.zeros((tq, HEAD_DIM), F32))
    for d in reversed(range(n_diag)):
        carry = step(i * n_diag + d, carry, d)
    def full_slabs(n, c):
        for u in range(n_diag):
            c = step((i - n) * n_diag - 1 - u, c, None)
        return c

    _, acc = lax.fori_loop(0, i, full_slabs, carry)
    o_ref[...] = acc.astype(o_ref.dtype)


def _stick_attention(qkv, *, batch, tq, tk):
    t, n3 = qkv.shape
    s = t // batch
    nh = n3 // (3 * HEAD_DIM)
    qkv3 = qkv.reshape(batch, s, n3)
    out = pl.pallas_call(
        functools.partial(_stick_kernel, tq=tq, tk=tk),
        grid=(batch, nh, s // tq),
        in_specs=[
            pl.BlockSpec((None, tq, HEAD_DIM), lambda b, h, i: (b, i, h)),
            pl.BlockSpec((None, s, HEAD_DIM), lambda b, h, i: (b, 0, nh + h)),
            pl.BlockSpec((None, s, HEAD_DIM), lambda b, h, i: (b, 0, 2 * nh + h)),
        ],
        out_specs=pl.BlockSpec((None, tq, HEAD_DIM), lambda b, h, i: (b, i, h)),
        out_shape=jax.ShapeDtypeStruct((batch, s, nh * HEAD_DIM), BF16),
        compiler_params=_params("parallel", "parallel", "arbitrary"),
    )(qkv3, qkv3, qkv3)
    return out.reshape(t, nh * HEAD_DIM)


def _t5_bucket_np(dist):
    max_exact = REL_BUCKETS // 2
    d = np.maximum(dist, 1).astype(np.float32)
    ratio = np.log(d / np.float32(max_exact)) / np.float32(math.log(REL_MAX_DIST / max_exact))
    large = max_exact + (ratio * np.float32(REL_BUCKETS - max_exact)).astype(np.int32)
    large = np.minimum(large, REL_BUCKETS - 1)
    return np.where(dist < max_exact, dist, large).astype(np.int32)


def _dilated_bucket_tables():
    qb = ATT_BLOCK
    qi = np.arange(qb)[:, None]
    kj = np.arange(2 * qb)[None, :]
    off = qb + qi - kj
    tables = []
    for window, dilation in DILATED_BRANCHES:
        span = window // dilation
        band = (off >= 0) & (off <= span)
        bucket = _t5_bucket_np(np.clip(off, 0, None) * dilation)
        tables.append(np.where(band, bucket, -1))
    return np.stack(tables).astype(np.int32)


def _dilated_kernel(rel_ref, idx_ref, q_ref, k_ref, v_ref, o_ref, bias_ref, ob_ref, lse_ref):
    h = pl.program_id(0)
    qb = ATT_BLOCK
    seq = q_ref.shape[0]

    @pl.when(pl.program_id(1) == 0)
    def _():
        for g in range(len(DILATED_BRANCHES)):
            idx = idx_ref[g]
            bias = jnp.full(idx.shape, NEG_BIG, F32)
            for b in range(REL_BUCKETS):
                bias = jnp.where(idx == b, rel_ref[h, b], bias)
            bias_ref[g] = bias

    nblk = BLOCKS_PER_STEP
    for g, (_, dil) in enumerate(DILATED_BRANCHES):
        nb = (seq // dil) // qb
        has_prev = nb > 1

        def step(c, carry, g=g, dil=dil, nb=nb, has_prev=has_prev):
            if has_prev:
                r = c // (nb // nblk)
                first = (c % (nb // nblk)) * nblk
                start = first * (qb * dil) + r
                if dil == 1:
                    row_sets = [pl.ds(pl.multiple_of(start, nblk * qb), nblk * qb)]
                else:
                    row_sets = [pl.ds(start, nblk * qb, stride=dil)]
            else:
                row_sets = [pl.ds(c * nblk + u, qb, stride=dil) for u in range(nblk)]

            def load(ref):
                parts = [ref[rs, :] for rs in row_sets]
                return (parts[0] if len(parts) == 1 else jnp.concatenate(parts, axis=0)).astype(BF16)

            q, k, v = load(q_ref), load(k_ref), load(v_ref)
            if has_prev:
                before = jnp.maximum(start - qb * dil, r)
                if dil == 1:
                    rs0 = pl.ds(pl.multiple_of(before, qb), qb)
                else:
                    rs0 = pl.ds(before, qb, stride=dil)
                k_prev = jnp.concatenate([k_ref[rs0, :].astype(BF16), k[:-qb]], axis=0)
                v_prev = jnp.concatenate([v_ref[rs0, :].astype(BF16), v[:-qb]], axis=0)
                bias = bias_ref[g]
                col = lax.broadcasted_iota(jnp.int32, (qb, 2 * qb), 1)
            else:
                bias = bias_ref[g, :, qb:]
            scores = []
            for b in range(nblk):
                rb = slice(b * qb, (b + 1) * qb)
                kb = jnp.concatenate([k_prev[rb], k[rb]], axis=0) if has_prev else k[rb]
                sb = lax.dot_general(q[rb], kb, (((1,), (1,)), ((), ())),
                                     preferred_element_type=F32) + bias
                if has_prev and b == 0:
                    sb = jnp.where(jnp.logical_and(first == 0, col < qb), NEG_BIG, sb)
                scores.append(sb)
            s = jnp.concatenate(scores, axis=0)
            m = jnp.max(s, axis=1, keepdims=True)
            e = jnp.exp(s - m)
            den = jnp.sum(e, axis=1, keepdims=True)
            e = e.astype(BF16)
            outs = []
            for b in range(nblk):
                rb = slice(b * qb, (b + 1) * qb)
                vb = jnp.concatenate([v_prev[rb], v[rb]], axis=0) if has_prev else v[rb]
                outs.append(jnp.dot(e[rb], vb, preferred_element_type=F32))
            out = jnp.concatenate(outs, axis=0) / den
            lse = jnp.broadcast_to(m + jnp.log(den), (nblk * qb, HEAD_DIM))
            rows_per_set = (nblk * qb) // len(row_sets)
            for u, rs in enumerate(row_sets):
                ru = slice(u * rows_per_set, (u + 1) * rows_per_set)
                ob_ref[g, rs, :] = out[ru]
                lse_ref[g, rs, :] = lse[ru]
            return carry

        lax.fori_loop(0, (dil * nb) // nblk, step, 0)

    def merge(c, carry):
        rs = pl.ds(pl.multiple_of(c * qb, qb), qb)
        lse = [lse_ref[g, rs, :] for g in range(len(DILATED_BRANCHES))]
        top = functools.reduce(jnp.maximum, lse)
        w = [jnp.exp(x - top) for x in lse]
        num = sum(w[g] * ob_ref[g, rs, :] for g in range(len(DILATED_BRANCHES)))
        o_ref[rs, :] = (num / sum(w)).astype(o_ref.dtype)
        return carry

    lax.fori_loop(0, seq // qb, merge, 0)


def _dilated_attention(qkv, rel_bias, *, batch):
    t, n3 = qkv.shape
    s = t // batch
    nh = n3 // (3 * HEAD_DIM)
    ng = len(DILATED_BRANCHES)
    qkv3 = qkv.reshape(batch, s, n3)
    idx = jnp.asarray(_dilated_bucket_tables())
    out = pl.pallas_call(
        _dilated_kernel,
        grid=(nh, batch),
        in_specs=[
            pl.BlockSpec(memory_space=pltpu.SMEM),
            pl.BlockSpec(idx.shape, lambda h, b: (0, 0, 0)),
            pl.BlockSpec((None, s, HEAD_DIM), lambda h, b: (b, 0, h)),
            pl.BlockSpec((None, s, HEAD_DIM), lambda h, b: (b, 0, nh + h)),
            pl.BlockSpec((None, s, HEAD_DIM), lambda h, b: (b, 0, 2 * nh + h)),
        ],
        out_specs=pl.BlockSpec((None, s, HEAD_DIM), lambda h, b: (b, 0, h)),
        out_shape=jax.ShapeDtypeStruct((batch, s, nh * HEAD_DIM), BF16),
        scratch_shapes=[pltpu.VMEM((ng, ATT_BLOCK, 2 * ATT_BLOCK), F32),
                        pltpu.VMEM((ng, s, HEAD_DIM), F32),
                        pltpu.VMEM((ng, s, HEAD_DIM), F32)],
        compiler_params=_params("arbitrary", "arbitrary"),
    )(rel_bias, idx, qkv3, qkv3, qkv3)
    return out.reshape(t, nh * HEAD_DIM)


def _final_norm_kernel(x_ref, g_ref, o_ref):
    _rmsnorm_rows(o_ref, x_ref, g_ref)


def _final_norm(x, g, *, tm):
    t, d = x.shape
    return pl.pallas_call(
        _final_norm_kernel,
        grid=(t // tm,),
        in_specs=[pl.BlockSpec((tm, d), lambda i: (i, 0)),
                  pl.BlockSpec((1, d), lambda i: (0, 0))],
        out_specs=pl.BlockSpec((tm, d), lambda i: (i, 0)),
        out_shape=jax.ShapeDtypeStruct((t, d), F32),
        compiler_params=_params("parallel"),
    )(x, g.reshape(1, d))


def _tiles(t, k_dim=None):
    tm = min(1024, t)
    return dict(tm=tm, tn=1024 if k_dim is not None and k_dim <= 2048 else 512)


def _ffn(x, g, wg, wu, wd):
    tiles = _tiles(x.shape[0])
    a = _ffn_up(x, g, wg, wu, **tiles)
    return _matmul_residual(a, wd, x, scale=FFN_HALF, **tiles)


def _split_forget(w_in, b_f):
    d3 = w_in.shape[1] - N_HEADS
    wf = jnp.pad(w_in[:, d3:], ((0, 0), (0, HEAD_DIM - N_HEADS))).astype(BF16)
    bf = jnp.pad(b_f, (0, HEAD_DIM - N_HEADS)).reshape(1, HEAD_DIM).astype(F32)
    return w_in[:, :d3].astype(BF16), wf, bf


def kernel(x, norm_g, ffn_w_gate, ffn_w_up, ffn_w_down, w_in_0, b_f_0, w_in_1, w_in_2,
           w_in_3, b_f_3, w_out, rel_bias, final_g):
    batch, seq, d = x.shape
    t = batch * seq
    depth = norm_g.shape[0]
    tiles = _tiles(t, k_dim=d)
    wg = ffn_w_gate.astype(BF16)
    wu = ffn_w_up.astype(BF16)
    wd = ffn_w_down.astype(BF16)
    wo = w_out.astype(BF16)
    w_ins = (w_in_0, w_in_1, w_in_2, w_in_3)
    forget_biases = {0: b_f_0, 3: b_f_3}

    xt = x.reshape(t, d)
    for i in range(depth):
        xt = _ffn(xt, norm_g[i, 0], wg[i, 0], wu[i, 0], wd[i, 0])
        mixer = i % 3
        if mixer == 0:
            w_qkv, wf, bf = _split_forget(w_ins[i], forget_biases[i])
            qkv, f = _qkv_proj(xt, norm_g[i, 1], w_qkv, wf, out_dtype=BF16, **tiles)
            ccol = _forget_cumsum(f, bf, batch=batch)
            crow = ccol.reshape(batch, seq, HEAD_DIM)[:, :, :N_HEADS]
            crow = jnp.transpose(crow, (0, 2, 1)).reshape(batch * N_HEADS, 1, seq)
            o = _fox_attention(qkv, ccol, crow, batch=batch, tq=1024, tk=512)
        elif mixer == 1:
            qkv = _qkv_proj(xt, norm_g[i, 1], w_ins[i].astype(BF16), out_dtype=F32, **tiles)
            o = _dilated_attention(qkv, rel_bias.astype(F32), batch=batch)
        else:
            qkv = _qkv_proj(xt, norm_g[i, 1], w_ins[i].astype(BF16), out_dtype=BF16, **tiles)
            o = _stick_attention(qkv, batch=batch, tq=1024, tk=256)
        xt = _matmul_residual(o, wo[i], xt, scale=1.0, **tiles)
        xt = _ffn(xt, norm_g[i, 2], wg[i, 1], wu[i, 1], wd[i, 1])
    return _final_norm(xt, final_g, tm=tiles["tm"]).reshape(batch, seq, d)
```

```python
import functools
import math

import numpy as np
import jax
import jax.numpy as jnp
from jax import lax
from jax.experimental import pallas as pl
from jax.experimental.pallas import tpu as pltpu

F32 = jnp.float32
BF16 = jnp.bfloat16

N_HEADS = 16
HEAD_DIM = 128
RMS_EPS = 1e-6
FFN_HALF = 0.5
REL_BUCKETS = 32
REL_MAX_DIST = 2048
ATT_BLOCK = 128
BLOCKS_PER_STEP = 4
DILATED_BRANCHES = ((128, 1), (512, 4), (2048, 16))
NEG_BIG = -1e30

VMEM_LIMIT_BYTES = 56 * 1024 * 1024
NORM_ROWS = 32
NORM_UNROLL = 4


def _params(*semantics):
    return pltpu.CompilerParams(dimension_semantics=semantics,
                                vmem_limit_bytes=VMEM_LIMIT_BYTES)


def _rmsnorm_rows(h_ref, x_ref, g_ref):
    rows = x_ref.shape[0]
    g = g_ref[...]

    def body(c, carry):
        r0 = pl.multiple_of(c * NORM_ROWS, NORM_ROWS)
        xc = x_ref[pl.ds(r0, NORM_ROWS), :]
        ms = jnp.mean(xc * xc, axis=-1, keepdims=True)
        h_ref[pl.ds(r0, NORM_ROWS), :] = ((xc * lax.rsqrt(ms + RMS_EPS)) * g).astype(h_ref.dtype)
        return carry

    lax.fori_loop(0, rows // NORM_ROWS, body, 0, unroll=NORM_UNROLL)


def _ffn_up_kernel(x_ref, g_ref, wg_ref, wu_ref, o_ref, h_ref):
    @pl.when(pl.program_id(1) == 0)
    def _():
        _rmsnorm_rows(h_ref, x_ref, g_ref)

    h = h_ref[...]
    gate = jnp.dot(h, wg_ref[...].astype(BF16), preferred_element_type=F32)
    up = jnp.dot(h, wu_ref[...].astype(BF16), preferred_element_type=F32)
    o_ref[...] = (gate * jax.nn.sigmoid(gate) * up).astype(o_ref.dtype)


def _ffn_up(x, g, wg, wu, *, tm, tn):
    t, d = x.shape
    f = wg.shape[1]
    return pl.pallas_call(
        _ffn_up_kernel,
        grid=(t // tm, f // tn),
        in_specs=[
            pl.BlockSpec((tm, d), lambda i, j: (i, 0)),
            pl.BlockSpec((1, d), lambda i, j: (0, 0)),
            pl.BlockSpec((d, tn), lambda i, j: (0, j)),
            pl.BlockSpec((d, tn), lambda i, j: (0, j)),
        ],
        out_specs=pl.BlockSpec((tm, tn), lambda i, j: (i, j)),
        out_shape=jax.ShapeDtypeStruct((t, f), BF16),
        scratch_shapes=[pltpu.VMEM((tm, d), BF16)],
        compiler_params=_params("parallel", "arbitrary"),
    )(x, g.reshape(1, d), wg, wu)


def _matmul_residual_kernel(a_ref, w_ref, r_ref, o_ref, *, scale):
    acc = jnp.dot(a_ref[...], w_ref[...], preferred_element_type=F32)
    o_ref[...] = r_ref[...] + scale * acc


def _matmul_residual(a, w, res, *, scale, tm, tn):
    t, k = a.shape
    n = w.shape[1]
    return pl.pallas_call(
        functools.partial(_matmul_residual_kernel, scale=scale),
        grid=(t // tm, n // tn),
        in_specs=[
            pl.BlockSpec((tm, k), lambda i, j: (i, 0)),
            pl.BlockSpec((k, tn), lambda i, j: (0, j)),
            pl.BlockSpec((tm, tn), lambda i, j: (i, j)),
        ],
        out_specs=pl.BlockSpec((tm, tn), lambda i, j: (i, j)),
        out_shape=jax.ShapeDtypeStruct((t, n), F32),
        compiler_params=_params("parallel", "arbitrary"),
    )(a, w, res)


def _qkv_kernel(x_ref, g_ref, w_ref, o_ref, h_ref, *, q_blocks, q_scale):
    j = pl.program_id(1)

    @pl.when(j == 0)
    def _():
        _rmsnorm_rows(h_ref, x_ref, g_ref)

    acc = jnp.dot(h_ref[...], w_ref[...], preferred_element_type=F32)
    scale = jnp.where(j < q_blocks, q_scale, 1.0).astype(F32)
    o_ref[...] = (acc * scale).astype(o_ref.dtype)


def _qkv_forget_kernel(x_ref, g_ref, w_ref, wf_ref, o_ref, f_ref, h_ref, *, q_blocks, q_scale):
    j = pl.program_id(1)

    @pl.when(j == 0)
    def _():
        _rmsnorm_rows(h_ref, x_ref, g_ref)
        f_ref[...] = jnp.dot(h_ref[...], wf_ref[...], preferred_element_type=F32)

    acc = jnp.dot(h_ref[...], w_ref[...], preferred_element_type=F32)
    scale = jnp.where(j < q_blocks, q_scale, 1.0).astype(F32)
    o_ref[...] = (acc * scale).astype(o_ref.dtype)


def _qkv_proj(x, g, w, wf=None, *, out_dtype, tm, tn):
    t, d = x.shape
    n = w.shape[1]
    kw = dict(q_blocks=(n // 3) // tn, q_scale=HEAD_DIM ** -0.5)
    in_specs = [
        pl.BlockSpec((tm, d), lambda i, j: (i, 0)),
        pl.BlockSpec((1, d), lambda i, j: (0, 0)),
        pl.BlockSpec((d, tn), lambda i, j: (0, j)),
    ]
    qkv_spec = pl.BlockSpec((tm, tn), lambda i, j: (i, j))
    qkv_shape = jax.ShapeDtypeStruct((t, n), out_dtype)
    scratch = [pltpu.VMEM((tm, d), BF16)]
    if wf is None:
        return pl.pallas_call(
            functools.partial(_qkv_kernel, **kw),
            grid=(t // tm, n // tn), in_specs=in_specs, out_specs=qkv_spec,
            out_shape=qkv_shape, scratch_shapes=scratch,
            compiler_params=_params("parallel", "arbitrary"),
        )(x, g.reshape(1, d), w)
    nf = wf.shape[1]
    return pl.pallas_call(
        functools.partial(_qkv_forget_kernel, **kw),
        grid=(t // tm, n // tn),
        in_specs=in_specs + [pl.BlockSpec((d, nf), lambda i, j: (0, 0))],
        out_specs=[qkv_spec, pl.BlockSpec((tm, nf), lambda i, j: (i, 0))],
        out_shape=[qkv_shape, jax.ShapeDtypeStruct((t, nf), F32)],
        scratch_shapes=scratch,
        compiler_params=_params("parallel", "arbitrary"),
    )(x, g.reshape(1, d), w, wf)


CUMSUM_ROWS = 256


def _split3(x):
    hi = x.astype(BF16)
    r1 = x - hi.astype(F32)
    mid = r1.astype(BF16)
    lo = (r1 - mid.astype(F32)).astype(BF16)
    return hi, mid, lo


def _forget_cumsum_kernel(f_ref, b_ref, c_ref):
    s = f_ref.shape[0]
    row = lax.broadcasted_iota(jnp.int32, (CUMSUM_ROWS, CUMSUM_ROWS), 0)
    col = lax.broadcasted_iota(jnp.int32, (CUMSUM_ROWS, CUMSUM_ROWS), 1)
    tri = (col <= row).astype(BF16)
    carry = jnp.zeros((1, f_ref.shape[1]), F32)
    for c in range(s // CUMSUM_ROWS):
        z = f_ref[pl.ds(c * CUMSUM_ROWS, CUMSUM_ROWS), :] + b_ref[...]
        log_f = jnp.minimum(z, 0.0) - jnp.log(1.0 + jnp.exp(-jnp.abs(z)))
        hi, mid, lo = _split3(log_f)
        part = (jnp.dot(tri, hi, preferred_element_type=F32)
                + jnp.dot(tri, mid, preferred_element_type=F32)
                + jnp.dot(tri, lo, preferred_element_type=F32))
        out = part + carry
        c_ref[pl.ds(c * CUMSUM_ROWS, CUMSUM_ROWS), :] = out
        carry = out[CUMSUM_ROWS - 1:CUMSUM_ROWS, :]


def _forget_cumsum(f, b, *, batch):
    t, nf = f.shape
    s = t // batch
    return pl.pallas_call(
        _forget_cumsum_kernel,
        grid=(batch,),
        in_specs=[pl.BlockSpec((s, nf), lambda b_: (b_, 0)),
                  pl.BlockSpec((1, nf), lambda b_: (0, 0))],
        out_specs=pl.BlockSpec((s, nf), lambda b_: (b_, 0)),
        out_shape=jax.ShapeDtypeStruct((t, nf), F32),
        compiler_params=_params("parallel"),
    )(f, b)


def _fox_kernel(q_ref, k_ref, v_ref, ccol_ref, crow_ref, o_ref, *, tq, tk):
    h = pl.program_id(1)
    i = pl.program_id(2)
    q = q_ref[...]
    lane = lax.broadcasted_iota(jnp.int32, ccol_ref.shape, 1)
    cq = jnp.sum(jnp.where(lane == h, ccol_ref[...], 0.0), axis=1, keepdims=True)
    row = lax.broadcasted_iota(jnp.int32, (tq, tk), 0)
    col = lax.broadcasted_iota(jnp.int32, (tq, tk), 1)
    n_diag = tq // tk

    def step(j, carry, diag):
        m, l, acc = carry
        k0 = pl.multiple_of(j * tk, tk)
        k = k_ref[pl.ds(k0, tk), :]
        v = v_ref[pl.ds(k0, tk), :]
        s = lax.dot_general(q, k, (((1,), (1,)), ((), ())), preferred_element_type=F32)
        s = s + cq - crow_ref[:, pl.ds(k0, tk)]
        if diag is not None:
            s = jnp.where(col + diag * tk <= row, s, NEG_BIG)
        m_new = jnp.maximum(m, jnp.max(s, axis=1, keepdims=True))
        p = jnp.exp(s - m_new)
        alpha = jnp.exp(m - m_new)
        l = alpha * l + jnp.sum(p, axis=1, keepdims=True)
        acc = alpha * acc + jnp.dot(p.astype(BF16), v, preferred_element_type=F32)
        return m_new, l, acc

    carry = (jnp.full((tq, 1), NEG_BIG, F32), jnp.zeros((tq, 1), F32),
             jnp.zeros((tq, HEAD_DIM), F32))
    def full_blocks(n, c):
        for u in range(n_diag):
            c = step(n * n_diag + u, c, None)
        return c

    carry = lax.fori_loop(0, i, full_blocks, carry)
    for d in range(n_diag):
        carry = step(i * n_diag + d, carry, d)
    _, l, acc = carry
    o_ref[...] = (acc / l).astype(o_ref.dtype)


def _fox_attention(qkv, ccol, crow, *, batch, tq, tk):
    t, n3 = qkv.shape
    s = t // batch
    nh = n3 // (3 * HEAD_DIM)
    qkv3 = qkv.reshape(batch, s, n3)
    ccol3 = ccol.reshape(batch, s, ccol.shape[1])
    out = pl.pallas_call(
        functools.partial(_fox_kernel, tq=tq, tk=tk),
        grid=(batch, nh, s // tq),
        in_specs=[
            pl.BlockSpec((None, tq, HEAD_DIM), lambda b, h, i: (b, i, h)),
            pl.BlockSpec((None, s, HEAD_DIM), lambda b, h, i: (b, 0, nh + h)),
            pl.BlockSpec((None, s, HEAD_DIM), lambda b, h, i: (b, 0, 2 * nh + h)),
            pl.BlockSpec((None, tq, ccol.shape[1]), lambda b, h, i: (b, i, 0)),
            pl.BlockSpec((None, 1, s), lambda b, h, i: (b * nh + h, 0, 0)),
        ],
        out_specs=pl.BlockSpec((None, tq, HEAD_DIM), lambda b, h, i: (b, i, h)),
        out_shape=jax.ShapeDtypeStruct((batch, s, nh * HEAD_DIM), BF16),
        compiler_params=_params("parallel", "parallel", "arbitrary"),
    )(qkv3, qkv3, qkv3, ccol3, crow)
    return out.reshape(t, nh * HEAD_DIM)


SUFFIX_BLOCK = 128


def _stick_kernel(q_ref, k_ref, v_ref, o_ref, *, tq, tk):
    i = pl.program_id(2)
    sb = SUFFIX_BLOCK
    q = q_ref[...]
    jj = lax.broadcasted_iota(jnp.int32, (2 * sb, 2 * sb), 0) % sb
    ss = lax.broadcasted_iota(jnp.int32, (2 * sb, 2 * sb), 1)
    suffix = ((ss >= sb) | (jj > ss)).astype(BF16)
    n_diag = tq // tk

    def step(j, carry, diag):
        r0 = 0 if diag is None else diag * tk
        tail, acc = (x[r0:] for x in carry)
        k0 = pl.multiple_of(j * tk, tk)
        k = k_ref[pl.ds(k0, tk), :]
        v = v_ref[pl.ds(k0, tk), :]
        z = lax.dot_general(q[r0:], k, (((1,), (1,)), ((), ())), preferred_element_type=F32)
        log_beta = jnp.minimum(z, 0.0) - jnp.log(1.0 + jnp.exp(-jnp.abs(z)))
        log_rest = log_beta - z
        if diag is not None:
            causal = (lax.broadcasted_iota(jnp.int32, (tq - r0, tk), 1)
                      < lax.broadcasted_iota(jnp.int32, (tq - r0, tk), 0))
            log_rest = jnp.where(causal, log_rest, 0.0)
        hi = log_rest.astype(BF16)
        lo = (log_rest - hi.astype(F32)).astype(BF16)
        after = [None] * (tk // sb)
        for u in reversed(range(tk // sb)):
            cols = slice(u * sb, (u + 1) * sb)
            sums = jnp.dot(jnp.concatenate([hi[:, cols], lo[:, cols]], axis=1), suffix,
                           preferred_element_type=F32)
            after[u] = tail + sums[:, :sb]
            tail = tail + sums[:, sb:]
        a = jnp.exp(log_beta + jnp.concatenate(after, axis=1))
        if diag is not None:
            a = jnp.where(causal, a, 0.0)
        acc = acc + jnp.dot(a.astype(BF16), v, preferred_element_type=F32)
        new = (tail, acc)
        if r0:
            new = tuple(jnp.concatenate([old[:r0], part], axis=0) for old, part in zip(carry, new))
        return new

    carry = (jnp.zeros((tq, sb), F32), jnp.zeros((tq, HEAD_DIM), F32))
    for d in reversed(range(n_diag)):
        carry = step(i * n_diag + d, carry, d)
    def full_slabs(n, c):
        for u in range(n_diag):
            c = step((i - n) * n_diag - 1 - u, c, None)
        return c

    _, acc = lax.fori_loop(0, i, full_slabs, carry)
    o_ref[...] = acc.astype(o_ref.dtype)


def _stick_attention(qkv, *, batch, tq, tk):
    t, n3 = qkv.shape
    s = t // batch
    nh = n3 // (3 * HEAD_DIM)
    qkv3 = qkv.reshape(batch, s, n3)
    out = pl.pallas_call(
        functools.partial(_stick_kernel, tq=tq, tk=tk),
        grid=(batch, nh, s // tq),
        in_specs=[
            pl.BlockSpec((None, tq, HEAD_DIM), lambda b, h, i: (b, i, h)),
            pl.BlockSpec((None, s, HEAD_DIM), lambda b, h, i: (b, 0, nh + h)),
            pl.BlockSpec((None, s, HEAD_DIM), lambda b, h, i: (b, 0, 2 * nh + h)),
        ],
        out_specs=pl.BlockSpec((None, tq, HEAD_DIM), lambda b, h, i: (b, i, h)),
        out_shape=jax.ShapeDtypeStruct((batch, s, nh * HEAD_DIM), BF16),
        compiler_params=_params("parallel", "parallel", "arbitrary"),
    )(qkv3, qkv3, qkv3)
    return out.reshape(t, nh * HEAD_DIM)


def _t5_bucket_np(dist):
    max_exact = REL_BUCKETS // 2
    d = np.maximum(dist, 1).astype(np.float32)
    ratio = np.log(d / np.float32(max_exact)) / np.float32(math.log(REL_MAX_DIST / max_exact))
    large = max_exact + (ratio * np.float32(REL_BUCKETS - max_exact)).astype(np.int32)
    large = np.minimum(large, REL_BUCKETS - 1)
    return np.where(dist < max_exact, dist, large).astype(np.int32)


def _dilated_bucket_tables():
    qb = ATT_BLOCK
    qi = np.arange(qb)[:, None]
    kj = np.arange(2 * qb)[None, :]
    off = qb + qi - kj
    tables = []
    for window, dilation in DILATED_BRANCHES:
        span = window // dilation
        band = (off >= 0) & (off <= span)
        bucket = _t5_bucket_np(np.clip(off, 0, None) * dilation)
        tables.append(np.where(band, bucket, -1))
    return np.stack(tables).astype(np.int32)


def _dilated_kernel(rel_ref, idx_ref, q_ref, k_ref, v_ref, o_ref, bias_ref, ob_ref, lse_ref):
    h = pl.program_id(0)
    qb = ATT_BLOCK
    seq = q_ref.shape[0]

    @pl.when(pl.program_id(1) == 0)
    def _():
        for g in range(len(DILATED_BRANCHES)):
            idx = idx_ref[g]
            bias = jnp.full(idx.shape, NEG_BIG, F32)
            for b in range(REL_BUCKETS):
                bias = jnp.where(idx == b, rel_ref[h, b], bias)
            bias_ref[g] = bias

    nblk = BLOCKS_PER_STEP
    for g, (_, dil) in enumerate(DILATED_BRANCHES):
        nb = (seq // dil) // qb
        has_prev = nb > 1

        def step(c, carry, g=g, dil=dil, nb=nb, has_prev=has_prev):
            if has_prev:
                r = c // (nb // nblk)
                first = (c % (nb // nblk)) * nblk
                start = first * (qb * dil) + r
                if dil == 1:
                    row_sets = [pl.ds(pl.multiple_of(start, nblk * qb), nblk * qb)]
                else:
                    row_sets = [pl.ds(start, nblk * qb, stride=dil)]
            else:
                row_sets = [pl.ds(c * nblk + u, qb, stride=dil) for u in range(nblk)]

            def load(ref):
                parts = [ref[rs, :] for rs in row_sets]
                return (parts[0] if len(parts) == 1 else jnp.concatenate(parts, axis=0)).astype(BF16)

            q, k, v = load(q_ref), load(k_ref), load(v_ref)
            if has_prev:
                before = jnp.maximum(start - qb * dil, r)
                if dil == 1:
                    rs0 = pl.ds(pl.multiple_of(before, qb), qb)
                else:
                    rs0 = pl.ds(before, qb, stride=dil)
                k_prev = jnp.concatenate([k_ref[rs0, :].astype(BF16), k[:-qb]], axis=0)
                v_prev = jnp.concatenate([v_ref[rs0, :].astype(BF16), v[:-qb]], axis=0)
                bias = bias_ref[g]
                col = lax.broadcasted_iota(jnp.int32, (qb, 2 * qb), 1)
            else:
                bias = bias_ref[g, :, qb:]
            scores = []
            for b in range(nblk):
                rb = slice(b * qb, (b + 1) * qb)
                kb = jnp.concatenate([k_prev[rb], k[rb]], axis=0) if has_prev else k[rb]
                sb = lax.dot_general(q[rb], kb, (((1,), (1,)), ((), ())),
                                     preferred_element_type=F32) + bias
                if has_prev and b == 0:
                    sb = jnp.where(jnp.logical_and(first == 0, col < qb), NEG_BIG, sb)
                scores.append(sb)
            s = jnp.concatenate(scores, axis=0)
            m = jnp.max(s, axis=1, keepdims=True)
            e = jnp.exp(s - m)
            den = jnp.sum(e, axis=1, keepdims=True)
            e = e.astype(BF16)
            outs = []
            for b in range(nblk):
                rb = slice(b * qb, (b + 1) * qb)
                vb = jnp.concatenate([v_prev[rb], v[rb]], axis=0) if has_prev else v[rb]
                outs.append(jnp.dot(e[rb], vb, preferred_element_type=F32))
            out = jnp.concatenate(outs, axis=0) / den
            lse = jnp.broadcast_to(m + jnp.log(den), (nblk * qb, HEAD_DIM))
            rows_per_set = (nblk * qb) // len(row_sets)
            for u, rs in enumerate(row_sets):
                ru = slice(u * rows_per_set, (u + 1) * rows_per_set)
                ob_ref[g, rs, :] = out[ru]
                lse_ref[g, rs, :] = lse[ru]
            return carry

        lax.fori_loop(0, (dil * nb) // nblk, step, 0)

    def merge(c, carry):
        rs = pl.ds(pl.multiple_of(c * qb, qb), qb)
        lse = [lse_ref[g, rs, :] for g in range(len(DILATED_BRANCHES))]
        top = functools.reduce(jnp.maximum, lse)
        w = [jnp.exp(x - top) for x in lse]
        num = sum(w[g] * ob_ref[g, rs, :] for g in range(len(DILATED_BRANCHES)))
        o_ref[rs, :] = (num / sum(w)).astype(o_ref.dtype)
        return carry

    lax.fori_loop(0, seq // qb, merge, 0)


def _dilated_attention(qkv, rel_bias, *, batch):
    t, n3 = qkv.shape
    s = t // batch
    nh = n3 // (3 * HEAD_DIM)
    ng = len(DILATED_BRANCHES)
    qkv3 = qkv.reshape(batch, s, n3)
    idx = jnp.asarray(_dilated_bucket_tables())
    out = pl.pallas_call(
        _dilated_kernel,
        grid=(nh, batch),
        in_specs=[
            pl.BlockSpec(memory_space=pltpu.SMEM),
            pl.BlockSpec(idx.shape, lambda h, b: (0, 0, 0)),
            pl.BlockSpec((None, s, HEAD_DIM), lambda h, b: (b, 0, h)),
            pl.BlockSpec((None, s, HEAD_DIM), lambda h, b: (b, 0, nh + h)),
            pl.BlockSpec((None, s, HEAD_DIM), lambda h, b: (b, 0, 2 * nh + h)),
        ],
        out_specs=pl.BlockSpec((None, s, HEAD_DIM), lambda h, b: (b, 0, h)),
        out_shape=jax.ShapeDtypeStruct((batch, s, nh * HEAD_DIM), BF16),
        scratch_shapes=[pltpu.VMEM((ng, ATT_BLOCK, 2 * ATT_BLOCK), F32),
                        pltpu.VMEM((ng, s, HEAD_DIM), F32),
                        pltpu.VMEM((ng, s, HEAD_DIM), F32)],
        compiler_params=_params("arbitrary", "arbitrary"),
    )(rel_bias, idx, qkv3, qkv3, qkv3)
    return out.reshape(t, nh * HEAD_DIM)


def _final_norm_kernel(x_ref, g_ref, o_ref):
    _rmsnorm_rows(o_ref, x_ref, g_ref)


def _final_norm(x, g, *, tm):
    t, d = x.shape
    return pl.pallas_call(
        _final_norm_kernel,
        grid=(t // tm,),
        in_specs=[pl.BlockSpec((tm, d), lambda i: (i, 0)),
                  pl.BlockSpec((1, d), lambda i: (0, 0))],
        out_specs=pl.BlockSpec((tm, d), lambda i: (i, 0)),
        out_shape=jax.ShapeDtypeStruct((t, d), F32),
        compiler_params=_params("parallel"),
    )(x, g.reshape(1, d))


def _tiles(t, k_dim=None):
    tm = min(1024, t)
    return dict(tm=tm, tn=1024 if k_dim is not None and k_dim <= 2048 else 512)


def _ffn(x, g, wg, wu, wd):
    tiles = _tiles(x.shape[0])
    a = _ffn_up(x, g, wg, wu, **tiles)
    return _matmul_residual(a, wd, x, scale=FFN_HALF, **tiles)


def _split_forget(w_in, b_f):
    d3 = w_in.shape[1] - N_HEADS
    wf = jnp.pad(w_in[:, d3:], ((0, 0), (0, HEAD_DIM - N_HEADS))).astype(BF16)
    bf = jnp.pad(b_f, (0, HEAD_DIM - N_HEADS)).reshape(1, HEAD_DIM).astype(F32)
    return w_in[:, :d3].astype(BF16), wf, bf


def kernel(x, norm_g, ffn_w_gate, ffn_w_up, ffn_w_down, w_in_0, b_f_0, w_in_1, w_in_2,
           w_in_3, b_f_3, w_out, rel_bias, final_g):
    batch, seq, d = x.shape
    t = batch * seq
    depth = norm_g.shape[0]
    tiles = _tiles(t, k_dim=d)
    wg = ffn_w_gate
    wu = ffn_w_up
    wd = ffn_w_down.astype(BF16)
    wo = w_out.astype(BF16)
    w_ins = (w_in_0, w_in_1, w_in_2, w_in_3)
    forget_biases = {0: b_f_0, 3: b_f_3}

    xt = x.reshape(t, d)
    for i in range(depth):
        xt = _ffn(xt, norm_g[i, 0], wg[i, 0], wu[i, 0], wd[i, 0])
        mixer = i % 3
        if mixer == 0:
            w_qkv, wf, bf = _split_forget(w_ins[i], forget_biases[i])
            qkv, f = _qkv_proj(xt, norm_g[i, 1], w_qkv, wf, out_dtype=BF16, **tiles)
            ccol = _forget_cumsum(f, bf, batch=batch)
            crow = ccol.reshape(batch, seq, HEAD_DIM)[:, :, :N_HEADS]
            crow = jnp.transpose(crow, (0, 2, 1)).reshape(batch * N_HEADS, 1, seq)
            o = _fox_attention(qkv, ccol, crow, batch=batch, tq=1024, tk=512)
        elif mixer == 1:
            qkv = _qkv_proj(xt, norm_g[i, 1], w_ins[i].astype(BF16), out_dtype=F32, **tiles)
            o = _dilated_attention(qkv, rel_bias.astype(F32), batch=batch)
        else:
            qkv = _qkv_proj(xt, norm_g[i, 1], w_ins[i].astype(BF16), out_dtype=BF16, **tiles)
            o = _stick_attention(qkv, batch=batch, tq=1024, tk=256)
        xt = _matmul_residual(o, wo[i], xt, scale=1.0, **tiles)
        xt = _ffn(xt, norm_g[i, 2], wg[i, 1], wu[i, 1], wd[i, 1])
    return _final_norm(xt, final_g, tm=tiles["tm"]).reshape(batch, seq, d)
```
